```python
import math
import jax, jax.numpy as jnp
from jax import lax
import numpy as np

D_MODEL = 1024
BATCH = 8
SEQ = 4096
DEPTH = 2

N_META = 16
MLA_HEADS = 8
QK_NOPE = 128
QK_ROPE = 64
V_HEAD = 128
Q_RANK = 256
KV_RANK = 128
ROPE_THETA = 10000.0
Q_BLOCK = 128
LRU_WIDTH = D_MODEL
LRU_BLOCKS = 8
LRU_BLOCK_DIM = LRU_WIDTH // LRU_BLOCKS
LRU_CONV = 4
LRU_C = 8.0
D_FF = 2816
FFN_CONV = 3
DN_ALPHA = (2.0 * DEPTH) ** 0.25
DN_BETA = (8.0 * DEPTH) ** -0.25
LN_EPS = 1e-5
RMS_EPS = 1e-6

IN_PARTS = [Q_RANK, KV_RANK, QK_ROPE, LRU_WIDTH, LRU_WIDTH, D_MODEL, D_MODEL]
IN_COLS = sum(IN_PARTS)
IN_SPLITS = [int(v) for v in np.cumsum(IN_PARTS)[:-1]]

kernel_name = 'hybrid_mla_rglru_convffn_encoder'


def layer_norm(x, g, b):
    xf = x.astype(jnp.float32)
    mu = jnp.mean(xf, axis=-1, keepdims=True)
    xc = xf - mu
    var = jnp.mean(jnp.square(xc), axis=-1, keepdims=True)
    return (xc * lax.rsqrt(var + LN_EPS) * g.astype(jnp.float32) + b.astype(jnp.float32)).astype(x.dtype)


def rms_norm(x, g):
    xf = x.astype(jnp.float32)
    ms = jnp.mean(jnp.square(xf), axis=-1, keepdims=True)
    return (xf * lax.rsqrt(ms + RMS_EPS) * g.astype(jnp.float32)).astype(x.dtype)


def apply_rope(x, cos, sin):
    half = x.shape[-1] // 2
    x1, x2 = x[..., :half], x[..., half:]
    return jnp.concatenate([x1 * cos - x2 * sin, x2 * cos + x1 * sin], axis=-1)


def dwconv(x, w, b, pad_left):
    k, c = w.shape
    y = lax.conv_general_dilated(x, w[:, None, :].astype(x.dtype), window_strides=(1,),
                                 padding=[(pad_left, k - 1 - pad_left)],
                                 dimension_numbers=('NWC', 'WIO', 'NWC'), feature_group_count=c)
    return y + b.astype(x.dtype)


def mla_branch(cq, ckv, kr, q_norm, kv_norm, w_uq, w_uk, w_uv, cos, sin):
    bsz, t = cq.shape[0], cq.shape[1]
    cq = rms_norm(cq, q_norm)
    ckv = rms_norm(ckv, kv_norm)
    q = jnp.einsum('btr,rhd->bthd', cq, w_uq)
    q_nope = q[..., :QK_NOPE]
    q_rope = apply_rope(q[..., QK_NOPE:], cos[:, None, :], sin[:, None, :])
    k_nope = jnp.einsum('btr,rhd->bthd', ckv, w_uk)
    v = jnp.einsum('btr,rhd->bthd', ckv, w_uv)
    k_rope = apply_rope(kr, cos, sin)
    scale = 1.0 / math.sqrt(QK_NOPE + QK_ROPE)
    n_blk = -(-t // Q_BLOCK)
    pad = n_blk * Q_BLOCK - t

    def to_blocks(a):
        a = jnp.pad(a * scale, ((0, 0), (0, pad), (0, 0), (0, 0)))
        a = a.reshape(bsz, n_blk, Q_BLOCK, a.shape[2], a.shape[3])
        return jnp.moveaxis(a, 1, 0)

    def attend(blk):
        qn, qr = blk
        s = jnp.einsum('bqhd,bkhd->bhqk', qn, k_nope) + jnp.einsum('bqhr,bkr->bhqk', qr, k_rope)
        p = jax.nn.softmax(s.astype(jnp.float32), axis=-1).astype(v.dtype)
        return jnp.einsum('bhqk,bkhd->bqhd', p, v)

    o = lax.map(attend, (to_blocks(q_nope), to_blocks(q_rope)))
    o = jnp.moveaxis(o, 0, 1).reshape(bsz, n_blk * Q_BLOCK, MLA_HEADS * V_HEAD)
    return o[:, :t]


def _lin_comb(c1, c2):
    a1, b1 = c1
    a2, b2 = c2
    return a1 * a2, a2 * b1 + b2


def rg_lru(xc, w_rg, b_rg, w_ig, b_ig, lam):
    bsz, t, w = xc.shape
    xg = xc.reshape(bsz, t, LRU_BLOCKS, LRU_BLOCK_DIM)
    r = jax.nn.sigmoid(jnp.einsum('btgi,dgij->dbtgj', xg, w_rg).reshape(2, bsz, t, w) + b_rg[:, None, None, :])
    i = jax.nn.sigmoid(jnp.einsum('btgi,dgij->dbtgj', xg, w_ig).reshape(2, bsz, t, w) + b_ig[:, None, None, :])
    log_a = -LRU_C * r.astype(jnp.float32) * jax.nn.softplus(-lam.astype(jnp.float32))[:, None, None, :]
    a = jnp.exp(log_a)
    u = jnp.sqrt(-jnp.expm1(2.0 * log_a)) * (i * xc[None]).astype(jnp.float32)
    h_fwd = lax.associative_scan(_lin_comb, (a[0], u[0]), axis=1)[1]
    h_bwd = lax.associative_scan(_lin_comb, (a[1], u[1]), axis=1, reverse=True)[1]
    return (h_fwd + h_bwd).astype(xc.dtype)


def _normal(key, shape, fan_in, scale=1.0):
    return jax.random.normal(key, shape, jnp.float32) * (scale * fan_in ** -0.5)


def setup_inputs(seed: int = 0) -> dict:
    key = jax.random.key(seed)
    ks = jax.random.split(key, 32)
    L = DEPTH
    u = jax.random.uniform(ks[17], (L, 2, LRU_WIDTH), jnp.float32, 0.9, 0.999)
    s = u ** (1.0 / LRU_C)
    lam = jnp.log(s) - jnp.log1p(-s)
    return {
        'x': jax.random.normal(ks[0], (BATCH, SEQ, D_MODEL), jnp.float32),
        'meta_tokens': jax.random.normal(ks[1], (N_META, D_MODEL), jnp.float32),
        'ln0_g': 1.0 + 0.01 * jax.random.normal(ks[2], (D_MODEL,), jnp.float32),
        'ln0_b': 0.01 * jax.random.normal(ks[3], (D_MODEL,), jnp.float32),
        'w_in': _normal(ks[4], (L, D_MODEL, IN_COLS), D_MODEL),
        'q_norm': 1.0 + 0.01 * jax.random.normal(ks[5], (L, Q_RANK), jnp.float32),
        'kv_norm': 1.0 + 0.01 * jax.random.normal(ks[6], (L, KV_RANK), jnp.float32),
        'w_uq': _normal(ks[7], (L, Q_RANK, MLA_HEADS, QK_NOPE + QK_ROPE), Q_RANK),
        'w_uk': _normal(ks[8], (L, KV_RANK, MLA_HEADS, QK_NOPE), KV_RANK),
        'w_uv': _normal(ks[9], (L, KV_RANK, MLA_HEADS, V_HEAD), KV_RANK, DN_BETA),
        'w_o_mla': _normal(ks[10], (L, MLA_HEADS * V_HEAD, D_MODEL), MLA_HEADS * V_HEAD, DN_BETA),
        'lru_conv_w': _normal(ks[11], (L, LRU_CONV, LRU_WIDTH), LRU_CONV),
        'lru_conv_b': 0.01 * jax.random.normal(ks[12], (L, LRU_WIDTH), jnp.float32),
        'w_rg': _normal(ks[13], (L, 2, LRU_BLOCKS, LRU_BLOCK_DIM, LRU_BLOCK_DIM), LRU_BLOCK_DIM),
        'b_rg': 0.01 * jax.random.normal(ks[14], (L, 2, LRU_WIDTH), jnp.float32),
        'w_ig': _normal(ks[15], (L, 2, LRU_BLOCKS, LRU_BLOCK_DIM, LRU_BLOCK_DIM), LRU_BLOCK_DIM),
        'b_ig': 0.01 * jax.random.normal(ks[16], (L, 2, LRU_WIDTH), jnp.float32),
        'lru_lambda': lam,
        'w_o_lru': _normal(ks[18], (L, LRU_WIDTH, D_MODEL), LRU_WIDTH, DN_BETA),
        'w_out': _normal(ks[19], (L, D_MODEL, D_MODEL), D_MODEL, DN_BETA),
        'ln1_g': 1.0 + 0.01 * jax.random.normal(ks[20], (L, D_MODEL), jnp.float32),
        'ln1_b': 0.01 * jax.random.normal(ks[21], (L, D_MODEL), jnp.float32),
        'w_up': _normal(ks[22], (L, D_MODEL, 2 * D_FF), D_MODEL),
        'ffn_conv_w': _normal(ks[23], (L, FFN_CONV, 2 * D_FF), FFN_CONV),
        'ffn_conv_b': 0.01 * jax.random.normal(ks[24], (L, 2 * D_FF), jnp.float32),
        'w_down': _normal(ks[25], (L, D_FF, D_MODEL), D_FF, DN_BETA),
        'ln2_g': 1.0 + 0.01 * jax.random.normal(ks[26], (L, D_MODEL), jnp.float32),
        'ln2_b': 0.01 * jax.random.normal(ks[27], (L, D_MODEL), jnp.float32),
    }


def reference(x, meta_tokens, ln0_g, ln0_b, w_in, q_norm, kv_norm, w_uq, w_uk, w_uv, w_o_mla,
              lru_conv_w, lru_conv_b, w_rg, b_rg, w_ig, b_ig, lru_lambda, w_o_lru, w_out,
              ln1_g, ln1_b, w_up, ffn_conv_w, ffn_conv_b, w_down, ln2_g, ln2_b):
    bsz = x.shape[0]
    meta = jnp.broadcast_to(meta_tokens[None].astype(x.dtype), (bsz, N_META, x.shape[-1]))
    h = layer_norm(jnp.concatenate([meta, x], axis=1), ln0_g, ln0_b)
    t = h.shape[1]
    half = QK_ROPE // 2
    inv_freq = jnp.exp(-math.log(ROPE_THETA) * jnp.arange(half, dtype=jnp.float32) / half)
    ang = jnp.arange(t, dtype=jnp.float32)[:, None] * inv_freq[None, :]
    cos = jnp.cos(ang).astype(h.dtype)
    sin = jnp.sin(ang).astype(h.dtype)

    for l in range(DEPTH):
        proj = h @ w_in[l]
        cq, ckv, kr, lru_g, lru_x, g_mla, g_lru = jnp.split(proj, IN_SPLITS, axis=-1)
        y_mla = mla_branch(cq, ckv, kr, q_norm[l], kv_norm[l], w_uq[l], w_uk[l], w_uv[l], cos, sin) @ w_o_mla[l]
        xc = dwconv(lru_x, lru_conv_w[l], lru_conv_b[l], LRU_CONV // 2)
        y_lru = (jax.nn.gelu(lru_g) * rg_lru(xc, w_rg[l], b_rg[l], w_ig[l], b_ig[l], lru_lambda[l])) @ w_o_lru[l]
        z = jax.nn.sigmoid(g_mla) * y_mla + jax.nn.sigmoid(g_lru) * y_lru
        h = layer_norm(DN_ALPHA * h + z @ w_out[l], ln1_g[l], ln1_b[l])
        up = dwconv(h @ w_up[l], ffn_conv_w[l], ffn_conv_b[l], FFN_CONV // 2)
        gate, val = jnp.split(up, [D_FF], axis=-1)
        f = (jax.nn.gelu(gate) * val) @ w_down[l]
        h = layer_norm(DN_ALPHA * h + f, ln2_g[l], ln2_b[l])

    return h[:, N_META:]
```

```python
import functools
import math

import jax
import jax.numpy as jnp
from jax import lax
from jax.experimental import pallas as pl
from jax.experimental.pallas import tpu as pltpu

D_MODEL = 1024
N_META = 16
MLA_HEADS = 8
QK_NOPE = 128
QK_ROPE = 64
QK_DIM = QK_NOPE + QK_ROPE
V_HEAD = 128
V_AUG = 2 * V_HEAD
Q_RANK = 256
KV_RANK = 128
ROPE_THETA = 10000.0
LRU_BLOCKS = 8
LRU_BLOCK_DIM = 128
LRU_CONV = 4
LRU_C = 8.0
D_FF = 2816
FFN_CONV = 3
DEPTH = 2
DN_ALPHA = (2.0 * DEPTH) ** 0.25
LN_EPS = 1e-5
RMS_EPS = 1e-6

SMALL_COLS = Q_RANK + KV_RANK + 2 * QK_ROPE
GATE_COLS = 4 * D_MODEL

SUBLANES = 8
VMEM_LIMIT = 56 * 1024 * 1024

TM_PROJ = 512
TM_MIX = 512
TM_FFN = 1024
FF_CHUNK = 256
TQ = 512
TK = 256
HALO = 8
FFN_HALO = 16
META_CHUNK = 128

_BF16 = jnp.bfloat16
_F32 = jnp.float32


def _params(sem):
    return pltpu.CompilerParams(dimension_semantics=sem, vmem_limit_bytes=VMEM_LIMIT)


def _layer_norm(x, g, b):
    mu = jnp.mean(x, axis=-1, keepdims=True)
    xc = x - mu
    var = jnp.mean(xc * xc, axis=-1, keepdims=True)
    return xc * lax.rsqrt(var + LN_EPS) * g + b


def _rms_norm(x, g):
    ms = jnp.mean(x * x, axis=-1, keepdims=True)
    return x * lax.rsqrt(ms + RMS_EPS) * g


def _expm1(y):
    u = jnp.exp(y)
    near_zero = jnp.where(u == 1.0, y, (u - 1.0) * y / jnp.log(u))
    return jnp.where(u > 0.5, near_zero, u - 1.0)


def _dot(a, b):
    return jnp.dot(a, b, preferred_element_type=_F32)


def _dot_nt(a, b):
    return lax.dot_general(a, b, (((1,), (1,)), ((), ())), preferred_element_type=_F32)


def _ln0_kernel(x_ref, g_ref, b_ref, o_ref):
    o_ref[...] = _layer_norm(x_ref[...], g_ref[...], b_ref[...])


def _ln0(x, g, b, tm):
    n, d = x.shape
    return pl.pallas_call(
        _ln0_kernel,
        name="ln0",
        grid=(pl.cdiv(n, tm),),
        in_specs=[pl.BlockSpec((tm, d), lambda i: (i, 0)),
                  pl.BlockSpec((1, d), lambda i: (0, 0)),
                  pl.BlockSpec((1, d), lambda i: (0, 0))],
        out_specs=pl.BlockSpec((tm, d), lambda i: (i, 0)),
        out_shape=jax.ShapeDtypeStruct((n, d), _F32),
        compiler_params=_params(("parallel",)),
    )(x, g.reshape(1, d), b.reshape(1, d))


def _proj_kernel(h_ref, ws_ref, wg_ref, qn_ref, kvn_ref, wqn_ref, wqr_ref, wuk_ref, wuv_ref, cs_ref,
                 q_ref, k_ref, v_ref, g_ref, hb_ref):
    j = pl.program_id(1)

    @pl.when(j == 0)
    def _():
        hb = h_ref[...].astype(_BF16)
        hb_ref[...] = hb
        small = _dot(hb, ws_ref[...])
        cq = _rms_norm(small[:, :Q_RANK], qn_ref[...]).astype(_BF16)
        ckv = _rms_norm(small[:, Q_RANK:Q_RANK + KV_RANK], kvn_ref[...]).astype(_BF16)
        cs = cs_ref[...]
        scale = 1.0 / math.sqrt(QK_DIM)

        def rope(pair):
            prod = pair * cs
            return (prod + pltpu.roll(prod, QK_ROPE, axis=1))[:, :QK_ROPE]

        k_rope = rope(small[:, Q_RANK + KV_RANK:]).astype(_BF16)
        q_nope = _dot(cq, wqn_ref[...]) * scale
        q_rope = _dot(cq, wqr_ref[...]) * scale
        k_nope = _dot(ckv, wuk_ref[...])
        v = _dot(ckv, wuv_ref[...])
        lane = lax.broadcasted_iota(jnp.int32, (h_ref.shape[0], V_HEAD), 1)
        ones_col = jnp.where(lane == 0, 1.0, 0.0).astype(_BF16)
        for hd in range(MLA_HEADS):
            sl = slice(hd * QK_NOPE, (hd + 1) * QK_NOPE)
            q_ref[hd, :, :QK_NOPE] = q_nope[:, sl].astype(_BF16)
            q_ref[hd, :, QK_NOPE:] = rope(q_rope[:, sl]).astype(_BF16)
            k_ref[hd, :, :QK_NOPE] = k_nope[:, sl].astype(_BF16)
            k_ref[hd, :, QK_NOPE:] = k_rope
            v_ref[hd, :, :V_HEAD] = v[:, sl].astype(_BF16)
            v_ref[hd, :, V_HEAD:] = ones_col

    g_ref[...] = _dot(hb_ref[...], wg_ref[...])


def _proj(h, ws, wg, qn, kvn, wqn, wqr, wuk, wuv, cs):
    n, d = h.shape
    tm = TM_PROJ
    grid = (pl.cdiv(n, tm), GATE_COLS // D_MODEL)
    const = lambda i, j: (0, 0)
    head_out = lambda i, j: (0, i, 0)
    return pl.pallas_call(
        _proj_kernel,
        name="in_proj",
        grid=grid,
        in_specs=[pl.BlockSpec((tm, d), lambda i, j: (i, 0)),
                  pl.BlockSpec((d, SMALL_COLS), const),
                  pl.BlockSpec((d, D_MODEL), lambda i, j: (0, j)),
                  pl.BlockSpec((1, Q_RANK), const),
                  pl.BlockSpec((1, KV_RANK), const),
                  pl.BlockSpec((Q_RANK, MLA_HEADS * QK_NOPE), const),
                  pl.BlockSpec((Q_RANK, MLA_HEADS * 2 * QK_ROPE), const),
                  pl.BlockSpec((KV_RANK, MLA_HEADS * QK_NOPE), const),
                  pl.BlockSpec((KV_RANK, MLA_HEADS * V_HEAD), const),
                  pl.BlockSpec((tm, 2 * QK_ROPE), lambda i, j: (i, 0))],
        out_specs=[pl.BlockSpec((MLA_HEADS, tm, QK_DIM), head_out),
                   pl.BlockSpec((MLA_HEADS, tm, QK_DIM), head_out),
                   pl.BlockSpec((MLA_HEADS, tm, V_AUG), head_out),
                   pl.BlockSpec((tm, D_MODEL), lambda i, j: (i, j))],
        out_shape=[jax.ShapeDtypeStruct((MLA_HEADS, n, QK_DIM), _BF16),
                   jax.ShapeDtypeStruct((MLA_HEADS, n, QK_DIM), _BF16),
                   jax.ShapeDtypeStruct((MLA_HEADS, n, V_AUG), _BF16),
                   jax.ShapeDtypeStruct((n, GATE_COLS), _F32)],
        scratch_shapes=[pltpu.VMEM((tm, d), _BF16)],
        compiler_params=_params(("parallel", "arbitrary")),
    )(h, ws, wg, qn, kvn, wqn, wqr, wuk, wuv, cs)


def _attn_kernel(q_ref, k_ref, v_ref, o_ref, s_ref, *, t, n_meta):
    n_chunks = (t - n_meta) // TK

    def process(start, tq):
        q = q_ref[pl.ds(start, tq), :]
        lane = lax.broadcasted_iota(jnp.int32, (tq, META_CHUNK), 1)
        s_meta = jnp.where(lane < n_meta, _dot_nt(q, k_ref[0:META_CHUNK, :]), -jnp.inf)
        m_run = jnp.full((tq, TK), -jnp.inf, _F32)
        for c in range(n_chunks):
            s = _dot_nt(q, k_ref[n_meta + c * TK:n_meta + (c + 1) * TK, :])
            s_ref[c, 0:tq, :] = s
            m_run = jnp.maximum(m_run, s)
        m = jnp.maximum(jnp.max(m_run, axis=-1, keepdims=True),
                        jnp.max(s_meta, axis=-1, keepdims=True))
        acc = _dot(jnp.exp(s_meta - m).astype(_BF16), v_ref[0:META_CHUNK, :])
        for c in range(n_chunks):
            p = jnp.exp(s_ref[c, 0:tq, :] - m).astype(_BF16)
            acc = acc + _dot(p, v_ref[n_meta + c * TK:n_meta + (c + 1) * TK, :])
        o = acc[:, :V_HEAD] / acc[:, V_HEAD:V_HEAD + 1]
        o_ref[pl.ds(start, tq), :] = o.astype(o_ref.dtype)

    process(0, n_meta)

    def body(i, carry):
        process(pl.multiple_of(n_meta + i * TQ, 16), TQ)
        return carry

    lax.fori_loop(0, (t - n_meta) // TQ, body, 0)


def _attention(q, k, v, bsz, t):
    n = q.shape[1]
    kern = functools.partial(_attn_kernel, t=t, n_meta=N_META)
    seq = lambda b, hd: (hd, b, 0)
    return pl.pallas_call(
        kern,
        name="mla_attention",
        grid=(bsz, MLA_HEADS),
        in_specs=[pl.BlockSpec((None, t, QK_DIM), seq),
                  pl.BlockSpec((None, t, QK_DIM), seq),
                  pl.BlockSpec((None, t, V_AUG), seq)],
        out_specs=pl.BlockSpec((t, V_HEAD), lambda b, hd: (b, hd)),
        out_shape=jax.ShapeDtypeStruct((n, MLA_HEADS * V_HEAD), _BF16),
        scratch_shapes=[pltpu.VMEM(((t - N_META) // TK, TQ, TK), _F32)],
        compiler_params=_params(("parallel", "parallel")),
    )(q, k, v)


def _lru_kernel(g_ref, x_ref, cw_ref, cb_ref, wg_ref, bg_ref, lam_ref, y_ref,
                xp_ref, af_ref, uf_ref, ab_ref, ub_ref, *, t):
    bd = LRU_BLOCK_DIM
    n_chunk = SUBLANES
    clen = t // n_chunk

    xp_ref[0:HALO, :] = jnp.zeros((HALO, bd), _F32)
    xp_ref[HALO:HALO + t, :] = x_ref[...]
    xp_ref[HALO + t:HALO + t + HALO, :] = jnp.zeros((HALO, bd), _F32)
    xc = cb_ref[...]
    for tap in range(LRU_CONV):
        off = HALO + tap - LRU_CONV // 2
        xc = xc + xp_ref[off:off + t, :] * cw_ref[tap:tap + 1, :]

    gates = jax.nn.sigmoid(_dot(xc.astype(_BF16), wg_ref[...]) + bg_ref[...])
    lam = lam_ref[...]
    sp = jnp.maximum(-lam, 0.0) + jnp.log1p(jnp.exp(-jnp.abs(lam)))
    for d, (a_ref, u_ref) in enumerate(((af_ref, uf_ref), (ab_ref, ub_ref))):
        r = gates[:, d * bd:(d + 1) * bd]
        ig = gates[:, (2 + d) * bd:(3 + d) * bd]
        log_a = -LRU_C * r * sp[d:d + 1, :]
        a_ref[...] = jnp.exp(log_a)
        u_ref[...] = jnp.sqrt(-_expm1(2.0 * log_a)) * (ig * xc)

    def rows(j):
        return pl.ds(j, n_chunk, stride=clen)

    def pass1(j, carry):
        hf, pf, hb, pb = carry
        jb = clen - 1 - j
        a = af_ref[rows(j), :]
        hf = a * hf + uf_ref[rows(j), :]
        pf = pf * a
        uf_ref[rows(j), :] = hf
        af_ref[rows(j), :] = pf
        a = ab_ref[rows(jb), :]
        hb = a * hb + ub_ref[rows(jb), :]
        pb = pb * a
        ub_ref[rows(jb), :] = hb
        ab_ref[rows(jb), :] = pb
        return hf, pf, hb, pb

    zero = jnp.zeros((n_chunk, bd), _F32)
    one = jnp.ones((n_chunk, bd), _F32)
    hf, pf, hb, pb = lax.fori_loop(0, clen, pass1, (zero, one, zero, one))

    cf = [jnp.zeros((1, bd), _F32)]
    for c in range(1, n_chunk):
        cf.append(hf[c - 1:c, :] + pf[c - 1:c, :] * cf[c - 1])
    cb = [jnp.zeros((1, bd), _F32)]
    for c in range(n_chunk - 2, -1, -1):
        cb.insert(0, hb[c + 1:c + 2, :] + pb[c + 1:c + 2, :] * cb[0])
    cf = jnp.concatenate(cf, axis=0)
    cb = jnp.concatenate(cb, axis=0)

    def pass2(j, carry):
        uf_ref[rows(j), :] = (uf_ref[rows(j), :] + af_ref[rows(j), :] * cf
                              + ub_ref[rows(j), :] + ab_ref[rows(j), :] * cb)
        return carry

    lax.fori_loop(0, clen, pass2, 0)
    y_ref[...] = (jax.nn.gelu(g_ref[...]) * uf_ref[...]).astype(y_ref.dtype)


def _lru(gates, cw, cb, wg, bg, lam, bsz, t):
    n = gates.shape[0]
    bd = LRU_BLOCK_DIM
    kern = functools.partial(_lru_kernel, t=t)
    return pl.pallas_call(
        kern,
        name="rg_lru",
        grid=(bsz, LRU_BLOCKS),
        in_specs=[pl.BlockSpec((t, bd), lambda b, g: (b, g)),
                  pl.BlockSpec((t, bd), lambda b, g: (b, LRU_BLOCKS + g)),
                  pl.BlockSpec((LRU_CONV, bd), lambda b, g: (0, g)),
                  pl.BlockSpec((1, bd), lambda b, g: (0, g)),
                  pl.BlockSpec((None, bd, 4 * bd), lambda b, g: (g, 0, 0)),
                  pl.BlockSpec((None, 1, 4 * bd), lambda b, g: (g, 0, 0)),
                  pl.BlockSpec((2, bd), lambda b, g: (0, g))],
        out_specs=pl.BlockSpec((t, bd), lambda b, g: (b, g)),
        out_shape=jax.ShapeDtypeStruct((n, D_MODEL), _BF16),
        scratch_shapes=[pltpu.VMEM((t + 2 * HALO, bd), _F32)] + [pltpu.VMEM((t, bd), _F32)] * 4,
        compiler_params=_params(("parallel", "parallel")),
    )(gates, gates, cw, cb, wg, bg, lam)


def _mix_kernel(o_ref, y_ref, gm_ref, gl_ref, h_ref, wom_ref, wol_ref, wout_ref, g_ref, b_ref, out_ref):
    y_mla = _dot(o_ref[...], wom_ref[...])
    y_lru = _dot(y_ref[...], wol_ref[...])
    z = jax.nn.sigmoid(gm_ref[...]) * y_mla + jax.nn.sigmoid(gl_ref[...]) * y_lru
    r = DN_ALPHA * h_ref[...] + _dot(z.astype(_BF16), wout_ref[...])
    out_ref[...] = _layer_norm(r, g_ref[...], b_ref[...])


def _mix(o, y, gates, h, wom, wol, wout, g, b):
    n, d = h.shape
    tm = TM_MIX
    tile = lambda i: (i, 0)
    const = lambda i: (0, 0)
    return pl.pallas_call(
        _mix_kernel,
        name="mixer_out",
        grid=(pl.cdiv(n, tm),),
        in_specs=[pl.BlockSpec((tm, d), tile),
                  pl.BlockSpec((tm, d), tile),
                  pl.BlockSpec((tm, d), lambda i: (i, 2)),
                  pl.BlockSpec((tm, d), lambda i: (i, 3)),
                  pl.BlockSpec((tm, d), tile),
                  pl.BlockSpec((d, d), const),
                  pl.BlockSpec((d, d), const),
                  pl.BlockSpec((d, d), const),
                  pl.BlockSpec((1, d), const),
                  pl.BlockSpec((1, d), const)],
        out_specs=pl.BlockSpec((tm, d), tile),
        out_shape=jax.ShapeDtypeStruct((n, d), _F32),
        compiler_params=_params(("parallel",)),
    )(o, y, gates, gates, h, wom, wol, wout, g, b)


def _ffn_kernel(h_ref, hp_ref, hn_ref, wug_ref, wuv_ref, cwg_ref, cwv_ref, cbg_ref, cbv_ref, wd_ref,
                g_ref, b_ref, out_ref, lhs_ref, acc_ref, *, t, tm):
    halo = FFN_HALO
    i = pl.program_id(0)
    j = pl.program_id(1)

    @pl.when(j == 0)
    def _():
        lhs_ref[0:halo, :] = hp_ref[...].astype(_BF16)
        lhs_ref[halo:halo + tm, :] = h_ref[...].astype(_BF16)
        lhs_ref[halo + tm:, :] = hn_ref[...].astype(_BF16)
        acc_ref[...] = jnp.zeros_like(acc_ref)

    r0 = i * tm
    bnd = ((r0 + t - 1) // t) * t - r0
    row = lax.broadcasted_iota(jnp.int32, (tm, 1), 0)
    seq_first = row == bnd
    seq_last = row == bnd - 1

    lhs = lhs_ref[...]

    def conv(w_ref, cw_ref, cb_ref):
        up = _dot(lhs, w_ref[...])
        cw = cw_ref[...]
        return (jnp.where(seq_first, 0.0, up[halo - 1:halo - 1 + tm]) * cw[0:1]
                + up[halo:halo + tm] * cw[1:2]
                + jnp.where(seq_last, 0.0, up[halo + 1:halo + 1 + tm]) * cw[2:3]
                + cb_ref[...])

    gate = conv(wug_ref, cwg_ref, cbg_ref)
    val = conv(wuv_ref, cwv_ref, cbv_ref)
    act = (jax.nn.gelu(gate) * val).astype(_BF16)
    acc_ref[...] += _dot(act, wd_ref[...])

    @pl.when(j == pl.num_programs(1) - 1)
    def _():
        r = DN_ALPHA * h_ref[...] + acc_ref[...]
        out_ref[...] = _layer_norm(r, g_ref[...], b_ref[...])


def _ffn(h, wup, cw, cb, wd, g, b, t):
    n, d = h.shape
    tm = TM_FFN
    ck = FF_CHUNK
    nj = D_FF // ck
    halo = FFN_HALO
    hb = tm // halo
    last_halo = n // halo - 1
    kern = functools.partial(_ffn_kernel, t=t, tm=tm)
    const = lambda i, j: (0, 0)
    return pl.pallas_call(
        kern,
        name="convffn",
        grid=(pl.cdiv(n, tm), nj),
        in_specs=[pl.BlockSpec((tm, d), lambda i, j: (i, 0)),
                  pl.BlockSpec((halo, d), lambda i, j: (jnp.maximum(i * hb - 1, 0), 0)),
                  pl.BlockSpec((halo, d), lambda i, j: (jnp.minimum((i + 1) * hb, last_halo), 0)),
                  pl.BlockSpec((d, ck), lambda i, j: (0, j)),
                  pl.BlockSpec((d, ck), lambda i, j: (0, nj + j)),
                  pl.BlockSpec((FFN_CONV, ck), lambda i, j: (0, j)),
                  pl.BlockSpec((FFN_CONV, ck), lambda i, j: (0, nj + j)),
                  pl.BlockSpec((1, ck), lambda i, j: (0, j)),
                  pl.BlockSpec((1, ck), lambda i, j: (0, nj + j)),
                  pl.BlockSpec((ck, d), lambda i, j: (j, 0)),
                  pl.BlockSpec((1, d), const),
                  pl.BlockSpec((1, d), const)],
        out_specs=pl.BlockSpec((tm, d), lambda i, j: (i, 0)),
        out_shape=jax.ShapeDtypeStruct((n, d), _F32),
        scratch_shapes=[pltpu.VMEM((tm + 2 * halo, d), _BF16), pltpu.VMEM((tm, d), _F32)],
        compiler_params=_params(("parallel", "arbitrary")),
    )(h, h, h, wup, wup, cw, cw, cb, cb, wd, g, b)


def _swap_halves(w):
    half = w.shape[-1] // 2
    return jnp.concatenate([w[..., half:], w[..., :half]], axis=-1)


def kernel(x, meta_tokens, ln0_g, ln0_b, w_in, q_norm, kv_norm, w_uq, w_uk, w_uv, w_o_mla, lru_conv_w, lru_conv_b, w_rg, b_rg, w_ig, b_ig, lru_lambda, w_o_lru, w_out, ln1_g, ln1_b, w_up, ffn_conv_w, ffn_conv_b, w_down, ln2_g, ln2_b):
    bsz, seq, d = x.shape
    t = N_META + seq
    n = bsz * t
    assert d == D_MODEL and t % FFN_HALO == 0 and (t - N_META) % TQ == 0 and N_META <= META_CHUNK

    meta = jnp.broadcast_to(meta_tokens[None].astype(x.dtype), (bsz, N_META, d))
    h = _ln0(jnp.concatenate([meta, x], axis=1).reshape(n, d), ln0_g, ln0_b, TM_FFN)

    half = QK_ROPE // 2
    inv_freq = jnp.exp(-math.log(ROPE_THETA) * jnp.arange(half, dtype=_F32) / half)
    ang = jnp.arange(t, dtype=_F32)[:, None] * inv_freq[None, :]
    cos, sin = jnp.cos(ang), jnp.sin(ang)
    cs = jnp.tile(jnp.concatenate([cos, cos, -sin, sin], axis=-1), (bsz, 1))

    for l in range(DEPTH):
        w_small = w_in[l][:, :Q_RANK + KV_RANK + QK_ROPE]
        w_kr = w_in[l][:, Q_RANK + KV_RANK:Q_RANK + KV_RANK + QK_ROPE]
        ws = jnp.concatenate([w_small, _swap_halves(w_kr)], axis=-1).astype(_BF16)
        wg = w_in[l][:, Q_RANK + KV_RANK + QK_ROPE:].astype(_BF16)
        wq_nope = w_uq[l][:, :, :QK_NOPE].reshape(Q_RANK, MLA_HEADS * QK_NOPE).astype(_BF16)
        wq_r = w_uq[l][:, :, QK_NOPE:]
        wq_rope = jnp.concatenate([wq_r, _swap_halves(wq_r)], axis=-1)
        wq_rope = wq_rope.reshape(Q_RANK, MLA_HEADS * 2 * QK_ROPE).astype(_BF16)
        wuk = w_uk[l].reshape(KV_RANK, MLA_HEADS * QK_NOPE).astype(_BF16)
        wuv = w_uv[l].reshape(KV_RANK, MLA_HEADS * V_HEAD).astype(_BF16)

        q, k, v, gates = _proj(h, ws, wg, q_norm[l].reshape(1, -1), kv_norm[l].reshape(1, -1),
                               wq_nope, wq_rope, wuk, wuv, cs)
        o = _attention(q, k, v, bsz, t)

        w_gate = jnp.concatenate([w_rg[l][0], w_rg[l][1], w_ig[l][0], w_ig[l][1]], axis=-1).astype(_BF16)
        b_gate = jnp.concatenate([b_rg[l].reshape(2, LRU_BLOCKS, LRU_BLOCK_DIM),
                                  b_ig[l].reshape(2, LRU_BLOCKS, LRU_BLOCK_DIM)], axis=0)
        b_gate = jnp.transpose(b_gate, (1, 0, 2)).reshape(LRU_BLOCKS, 1, 4 * LRU_BLOCK_DIM)
        y = _lru(gates, lru_conv_w[l], lru_conv_b[l].reshape(1, -1), w_gate, b_gate, lru_lambda[l], bsz, t)

        h = _mix(o, y, gates, h, w_o_mla[l].astype(_BF16), w_o_lru[l].astype(_BF16), w_out[l].astype(_BF16),
                 ln1_g[l].reshape(1, -1), ln1_b[l].reshape(1, -1))
        h = _ffn(h, w_up[l].astype(_BF16), ffn_conv_w[l], ffn_conv_b[l].reshape(1, -1),
                 w_down[l].astype(_BF16), ln2_g[l].reshape(1, -1), ln2_b[l].reshape(1, -1), t)

    return h.reshape(bsz, t, d)[:, N_META:]
```

```python
import functools
import math

import jax
import jax.numpy as jnp
from jax import lax
from jax.experimental import pallas as pl
from jax.experimental.pallas import tpu as pltpu

D_MODEL = 1024
N_META = 16
MLA_HEADS = 8
QK_NOPE = 128
QK_ROPE = 64
QK_DIM = QK_NOPE + QK_ROPE
V_HEAD = 128
Q_RANK = 256
KV_RANK = 128
ROPE_THETA = 10000.0
LRU_BLOCKS = 8
LRU_BLOCK_DIM = 128
LRU_CONV = 4
LRU_C = 8.0
D_FF = 2816
FFN_CONV = 3
DEPTH = 2
DN_ALPHA = (2.0 * DEPTH) ** 0.25
LN_EPS = 1e-5
RMS_EPS = 1e-6

SMALL_COLS = Q_RANK + KV_RANK + 2 * QK_ROPE
GATE_COLS = 4 * D_MODEL

LANES = 128
SUBLANES = 8
BF16_ROWS = 16
VMEM_LIMIT = 56 * 1024 * 1024

V_ROWS = V_HEAD + BF16_ROWS
TM_LN0 = 1024
TM_PROJ = 512
TM_MIX = 512
TM_FFN = 1024
FF_CHUNK = 256
TQ = 512
TK = 256
LRU_SLABS = 2
CONV_HALO = SUBLANES
FFN_HALO = BF16_ROWS

_BF16 = jnp.bfloat16
_F32 = jnp.float32


def _params(sem):
    return pltpu.CompilerParams(dimension_semantics=sem, vmem_limit_bytes=VMEM_LIMIT)


def _resident(shape):
    zeros = (0,) * len(shape)
    return pl.BlockSpec(shape, lambda *_: zeros, pipeline_mode=pl.Buffered(1))


def _layer_norm(x, g, b):
    mu = jnp.mean(x, axis=-1, keepdims=True)
    xc = x - mu
    var = jnp.mean(xc * xc, axis=-1, keepdims=True)
    return xc * lax.rsqrt(var + LN_EPS) * g + b


def _rms_norm(x, g):
    ms = jnp.mean(x * x, axis=-1, keepdims=True)
    return x * lax.rsqrt(ms + RMS_EPS) * g


def _expm1(y):
    u = jnp.exp(y)
    near_zero = jnp.where(u == 1.0, y, (u - 1.0) * y / jnp.log(u))
    return jnp.where(u > 0.5, near_zero, u - 1.0)


def _sqrt(x):
    return jnp.where(x > 0.0, x * lax.rsqrt(x), 0.0)


def _dot(a, b):
    return jnp.dot(a, b, preferred_element_type=_F32)


def _dot_nt(a, b):
    return lax.dot_general(a, b, (((1,), (1,)), ((), ())), preferred_element_type=_F32)


def _real_rows(i, tm, t, tp):
    pos = lax.rem(i * tm, tp) + lax.broadcasted_iota(jnp.int32, (tm, 1), 0)
    return (pos < t) | (pos >= tp)


def _ln0_kernel(x_ref, g_ref, b_ref, o_ref, *, t, tp):
    tm = x_ref.shape[0]
    y = _layer_norm(x_ref[...], g_ref[...], b_ref[...])
    o_ref[...] = jnp.where(_real_rows(pl.program_id(0), tm, t, tp), y, 0.0)


def _ln0(x, g, b, t, tp):
    n, d = x.shape
    tm = TM_LN0
    return pl.pallas_call(
        functools.partial(_ln0_kernel, t=t, tp=tp),
        name="ln0",
        grid=(n // tm,),
        in_specs=[pl.BlockSpec((tm, d), lambda i: (i, 0)), _resident((1, d)), _resident((1, d))],
        out_specs=pl.BlockSpec((tm, d), lambda i: (i, 0)),
        out_shape=jax.ShapeDtypeStruct((n, d), _F32),
        compiler_params=_params(("parallel",)),
    )(x, g, b)


def _proj_kernel(h_ref, ws_ref, wg_ref, qn_ref, kvn_ref, wqt_ref, wuk_ref, wuvt_ref, cs_ref, cst_ref,
                 qt_ref, k_ref, vt_ref, g_ref):
    tm = h_ref.shape[0]
    hb = h_ref[...].astype(_BF16)
    small = _dot(hb, ws_ref[...])
    cq = _rms_norm(small[:, :Q_RANK], qn_ref[...]).astype(_BF16)
    ckv = _rms_norm(small[:, Q_RANK:Q_RANK + KV_RANK], kvn_ref[...]).astype(_BF16)

    prod = small[:, Q_RANK + KV_RANK:] * cs_ref[...]
    k_rope = (prod + pltpu.roll(prod, QK_ROPE, axis=1))[:, :QK_ROPE].astype(_BF16)
    k_nope = _dot(ckv, wuk_ref[...])
    for hd in range(MLA_HEADS):
        k_ref[hd, :, :QK_NOPE] = k_nope[:, hd * QK_NOPE:(hd + 1) * QK_NOPE].astype(_BF16)
        k_ref[hd, :, QK_NOPE:] = k_rope

    half = QK_ROPE // 2
    qt = _dot_nt(wqt_ref[...], cq) * (1.0 / math.sqrt(QK_DIM))
    cos_t = cst_ref[0:half, :]
    sin_t = cst_ref[half:QK_ROPE, :]
    for hd in range(MLA_HEADS):
        r0 = hd * QK_DIM
        x1 = qt[r0 + QK_NOPE:r0 + QK_NOPE + half]
        x2 = qt[r0 + QK_NOPE + half:r0 + QK_DIM]
        qt_ref[hd, 0:QK_NOPE, :] = qt[r0:r0 + QK_NOPE].astype(_BF16)
        qt_ref[hd, QK_NOPE:QK_NOPE + half, :] = (x1 * cos_t - x2 * sin_t).astype(_BF16)
        qt_ref[hd, QK_NOPE + half:QK_DIM, :] = (x2 * cos_t + x1 * sin_t).astype(_BF16)

    vt = _dot_nt(wuvt_ref[...], ckv)
    row = lax.broadcasted_iota(jnp.int32, (V_ROWS - V_HEAD, tm), 0)
    ones_rows = jnp.where(row == 0, 1.0, 0.0).astype(_BF16)
    for hd in range(MLA_HEADS):
        vt_ref[hd, 0:V_HEAD, :] = vt[hd * V_HEAD:(hd + 1) * V_HEAD].astype(_BF16)
        vt_ref[hd, V_HEAD:V_ROWS, :] = ones_rows

    for c in range(GATE_COLS // D_MODEL):
        cols = slice(c * D_MODEL, (c + 1) * D_MODEL)
        g_ref[:, cols] = _dot(hb, wg_ref[:, cols]).astype(_BF16)


def _proj(h, ws, wg, qn, kvn, wqt, wuk, wuvt, cs, cst):
    n, d = h.shape
    tm = TM_PROJ
    heads = lambda i: (0, i, 0)
    heads_t = lambda i: (0, 0, i)
    return pl.pallas_call(
        _proj_kernel,
        name="in_proj",
        grid=(n // tm,),
        in_specs=[pl.BlockSpec((tm, d), lambda i: (i, 0)),
                  _resident(ws.shape), _resident(wg.shape), _resident(qn.shape), _resident(kvn.shape),
                  _resident(wqt.shape), _resident(wuk.shape), _resident(wuvt.shape),
                  pl.BlockSpec((tm, 2 * QK_ROPE), lambda i: (i, 0)),
                  pl.BlockSpec((QK_ROPE, tm), lambda i: (0, i))],
        out_specs=[pl.BlockSpec((MLA_HEADS, QK_DIM, tm), heads_t),
                   pl.BlockSpec((MLA_HEADS, tm, QK_DIM), heads),
                   pl.BlockSpec((MLA_HEADS, V_ROWS, tm), heads_t),
                   pl.BlockSpec((tm, GATE_COLS), lambda i: (i, 0))],
        out_shape=[jax.ShapeDtypeStruct((MLA_HEADS, QK_DIM, n), _BF16),
                   jax.ShapeDtypeStruct((MLA_HEADS, n, QK_DIM), _BF16),
                   jax.ShapeDtypeStruct((MLA_HEADS, V_ROWS, n), _BF16),
                   jax.ShapeDtypeStruct((n, GATE_COLS), _BF16)],
        compiler_params=_params(("parallel",)),
    )(h, ws, wg, qn, kvn, wqt, wuk, wuvt, cs, cst)


def _attn_kernel(qt_ref, k_ref, vt_ref, o_ref, s0_ref, s1_ref, *, t, tp):
    n_chunks = tp // TK
    tail_k = tp - n_chunks * TK
    n_tiles = tp // TQ
    tail_q = tp - n_tiles * TQ

    def scores(start, tq, s_ref):
        s_ref[:, 0:tq] = _dot(k_ref[...], qt_ref[:, pl.ds(start, tq)])
        s_ref[t:tp, 0:tq] = jnp.full((tp - t, tq), -jnp.inf, _F32)
        return jnp.max(s_ref[:, 0:tq], axis=0, keepdims=True)

    def values(start, tq, s_ref, m):
        acc = jnp.zeros((V_ROWS, tq), _F32)
        for r0, rows in [(c * TK, TK) for c in range(n_chunks)] + [(n_chunks * TK, tail_k)]:
            p = jnp.exp(s_ref[r0:r0 + rows, 0:tq] - m).astype(_BF16)
            acc = acc + _dot(vt_ref[:, r0:r0 + rows], p)
        o_t = acc[:V_HEAD] / acc[V_HEAD:V_HEAD + 1]
        o_ref[pl.ds(start, tq), :] = o_t.T.astype(o_ref.dtype)

    def tile(i):
        return pl.multiple_of(i * TQ, TQ)

    m0 = scores(0, TQ, s0_ref)

    def body(j, m0):
        m1 = scores(tile(2 * j + 1), TQ, s1_ref)
        values(tile(2 * j), TQ, s0_ref, m0)
        m0 = scores(tile(2 * j + 2), TQ, s0_ref)
        values(tile(2 * j + 1), TQ, s1_ref, m1)
        return m0

    m0 = lax.fori_loop(0, n_tiles // 2 - 1, body, m0)
    m1 = scores((n_tiles - 1) * TQ, TQ, s1_ref)
    values((n_tiles - 2) * TQ, TQ, s0_ref, m0)
    m0 = scores(n_tiles * TQ, tail_q, s0_ref)
    values((n_tiles - 1) * TQ, TQ, s1_ref, m1)
    values(n_tiles * TQ, tail_q, s0_ref, m0)


def _attention(qt, k, vt, bsz, t, tp):
    n = k.shape[1]
    assert (tp // TQ) % 2 == 0 and 0 < tp % TQ <= TQ and tp % TQ % LANES == 0 and tp % TK % BF16_ROWS == 0
    return pl.pallas_call(
        functools.partial(_attn_kernel, t=t, tp=tp),
        name="mla_attention",
        grid=(bsz, MLA_HEADS),
        in_specs=[pl.BlockSpec((None, QK_DIM, tp), lambda b, hd: (hd, 0, b)),
                  pl.BlockSpec((None, tp, QK_DIM), lambda b, hd: (hd, b, 0)),
                  pl.BlockSpec((None, V_ROWS, tp), lambda b, hd: (hd, 0, b))],
        out_specs=pl.BlockSpec((tp, V_HEAD), lambda b, hd: (b, hd)),
        out_shape=jax.ShapeDtypeStruct((n, MLA_HEADS * V_HEAD), _BF16),
        scratch_shapes=[pltpu.VMEM((tp, TQ), _F32)] * 2,
        compiler_params=_params(("parallel", "parallel")),
    )(qt, k, vt)


def _lru_kernel(g_ref, x_ref, cw_ref, cb_ref, wg_ref, bg_ref, lam_ref, y_ref,
                xp_ref, af_ref, uf_ref, ab_ref, ub_ref, *, t, tp):
    bd = LRU_BLOCK_DIM
    n_chunk = SUBLANES
    clen = t // n_chunk
    lam = lam_ref[...]
    sp = jnp.maximum(-lam, 0.0) + jnp.log1p(jnp.exp(-jnp.abs(lam)))

    for s in range(LRU_SLABS):
        lanes = slice(s * bd, (s + 1) * bd)
        xp_ref[0:CONV_HALO, :] = jnp.zeros((CONV_HALO, bd), _F32)
        xp_ref[CONV_HALO:CONV_HALO + tp, :] = x_ref[:, lanes].astype(_F32)
        xc = cb_ref[:, lanes]
        for tap in range(LRU_CONV):
            off = CONV_HALO + tap - LRU_CONV // 2
            xc = xc + xp_ref[off:off + t, :] * cw_ref[tap:tap + 1, lanes]

        gates = jax.nn.sigmoid(_dot(xc.astype(_BF16), wg_ref[s]) + bg_ref[s])
        for d, (a_ref, u_ref) in enumerate(((af_ref, uf_ref), (ab_ref, ub_ref))):
            r = gates[:, d * bd:(d + 1) * bd]
            ig = gates[:, (2 + d) * bd:(3 + d) * bd]
            log_a = -LRU_C * r * sp[d:d + 1, lanes]
            a_ref[s] = jnp.exp(log_a)
            u_ref[s] = _sqrt(-_expm1(2.0 * log_a)) * (ig * xc)

    def rows(j):
        return pl.ds(j, n_chunk, stride=clen)

    def pass1(j, carry):
        jb = clen - 1 - j
        out = []
        for s in range(LRU_SLABS):
            hf, pf, hb, pb = carry[4 * s:4 * s + 4]
            a = af_ref[s, rows(j), :]
            hf = a * hf + uf_ref[s, rows(j), :]
            pf = pf * a
            uf_ref[s, rows(j), :] = hf
            af_ref[s, rows(j), :] = pf
            a = ab_ref[s, rows(jb), :]
            hb = a * hb + ub_ref[s, rows(jb), :]
            pb = pb * a
            ub_ref[s, rows(jb), :] = hb
            ab_ref[s, rows(jb), :] = pb
            out += [hf, pf, hb, pb]
        return tuple(out)

    zero = jnp.zeros((n_chunk, bd), _F32)
    one = jnp.ones((n_chunk, bd), _F32)
    ends = lax.fori_loop(0, clen, pass1, (zero, one, zero, one) * LRU_SLABS)

    carries = []
    for s in range(LRU_SLABS):
        hf, pf, hb, pb = ends[4 * s:4 * s + 4]
        cf = [jnp.zeros((1, bd), _F32)]
        for c in range(1, n_chunk):
            cf.append(hf[c - 1:c, :] + pf[c - 1:c, :] * cf[c - 1])
        cb = [jnp.zeros((1, bd), _F32)]
        for c in range(n_chunk - 2, -1, -1):
            cb.insert(0, hb[c + 1:c + 2, :] + pb[c + 1:c + 2, :] * cb[0])
        carries.append((jnp.concatenate(cf, axis=0), jnp.concatenate(cb, axis=0)))

    def pass2(j, carry):
        for s in range(LRU_SLABS):
            cf, cb = carries[s]
            uf_ref[s, rows(j), :] = (uf_ref[s, rows(j), :] + af_ref[s, rows(j), :] * cf
                                     + ub_ref[s, rows(j), :] + ab_ref[s, rows(j), :] * cb)
        return carry

    lax.fori_loop(0, clen, pass2, 0)
    for s in range(LRU_SLABS):
        lanes = slice(s * bd, (s + 1) * bd)
        y_ref[0:t, lanes] = (jax.nn.gelu(g_ref[0:t, lanes].astype(_F32)) * uf_ref[s]).astype(y_ref.dtype)
    y_ref[t:tp, :] = jnp.zeros((tp - t, LRU_SLABS * bd), y_ref.dtype)


def _lru(gates, cw, cb, wg, bg, lam, bsz, t, tp):
    n = gates.shape[0]
    bd = LRU_BLOCK_DIM
    w = LRU_SLABS * bd
    nb = D_MODEL // w
    assert t % SUBLANES == 0
    return pl.pallas_call(
        functools.partial(_lru_kernel, t=t, tp=tp),
        name="rg_lru",
        grid=(bsz, nb),
        in_specs=[pl.BlockSpec((tp, w), lambda b, g: (b, g)),
                  pl.BlockSpec((tp, w), lambda b, g: (b, nb + g)),
                  pl.BlockSpec((LRU_CONV, w), lambda b, g: (0, g)),
                  pl.BlockSpec((1, w), lambda b, g: (0, g)),
                  pl.BlockSpec((LRU_SLABS, bd, 4 * bd), lambda b, g: (g, 0, 0)),
                  pl.BlockSpec((LRU_SLABS, 1, 4 * bd), lambda b, g: (g, 0, 0)),
                  pl.BlockSpec((2, w), lambda b, g: (0, g))],
        out_specs=pl.BlockSpec((tp, w), lambda b, g: (b, g)),
        out_shape=jax.ShapeDtypeStruct((n, D_MODEL), _BF16),
        scratch_shapes=[pltpu.VMEM((tp + CONV_HALO, bd), _F32)] + [pltpu.VMEM((LRU_SLABS, t, bd), _F32)] * 4,
        compiler_params=_params(("parallel", "parallel")),
    )(gates, gates, cw, cb, wg, bg, lam)


def _mix_kernel(o_ref, y_ref, gm_ref, gl_ref, h_ref, wom_ref, wol_ref, wout_ref, g_ref, b_ref, out_ref, *, t, tp):
    tm = h_ref.shape[0]
    y_mla = _dot(o_ref[...], wom_ref[...])
    y_lru = _dot(y_ref[...], wol_ref[...])
    z = (jax.nn.sigmoid(gm_ref[...].astype(_F32)) * y_mla
         + jax.nn.sigmoid(gl_ref[...].astype(_F32)) * y_lru)
    r = DN_ALPHA * h_ref[...] + _dot(z.astype(_BF16), wout_ref[...])
    out = _layer_norm(r, g_ref[...], b_ref[...])
    out_ref[...] = jnp.where(_real_rows(pl.program_id(0), tm, t, tp), out, 0.0)


def _mix(o, y, gates, h, wom, wol, wout, g, b, t, tp):
    n, d = h.shape
    tm = TM_MIX
    tile = lambda i: (i, 0)
    return pl.pallas_call(
        functools.partial(_mix_kernel, t=t, tp=tp),
        name="mixer_out",
        grid=(n // tm,),
        in_specs=[pl.BlockSpec((tm, d), tile),
                  pl.BlockSpec((tm, d), tile),
                  pl.BlockSpec((tm, d), lambda i: (i, 2)),
                  pl.BlockSpec((tm, d), lambda i: (i, 3)),
                  pl.BlockSpec((tm, d), tile),
                  _resident((d, d)), _resident((d, d)), _resident((d, d)),
                  _resident((1, d)), _resident((1, d))],
        out_specs=pl.BlockSpec((tm, d), tile),
        out_shape=jax.ShapeDtypeStruct((n, d), _F32),
        compiler_params=_params(("parallel",)),
    )(o, y, gates, gates, h, wom, wol, wout, g, b)


def _ffn_kernel(h_ref, hp_ref, hn_ref, wu_ref, cw_ref, cb_ref, wd_ref, g_ref, b_ref, out_ref,
                lhs_ref, up_ref, acc_ref, *, t, tp):
    tm = h_ref.shape[0]
    halo = FFN_HALO
    i = pl.program_id(0)
    j = pl.program_id(1)

    @pl.when(j == 0)
    def _():
        lhs_ref[0:halo, :] = hp_ref[...].astype(_BF16)
        lhs_ref[halo:halo + tm, :] = h_ref[...].astype(_BF16)
        lhs_ref[halo + tm:, :] = hn_ref[...].astype(_BF16)
        acc_ref[...] = jnp.zeros_like(acc_ref)

        @pl.when(i == 0)
        def _():
            lhs_ref[0:halo, :] = jnp.zeros((halo, lhs_ref.shape[1]), _BF16)

    up_ref[...] = _dot(lhs_ref[...], wu_ref[...])
    cw = cw_ref[...]
    conv = cb_ref[...]
    for tap in range(FFN_CONV):
        off = halo + tap - FFN_CONV // 2
        conv = conv + up_ref[off:off + tm, :] * cw[tap:tap + 1, :]
    ck = conv.shape[1] // 2
    act = (jax.nn.gelu(conv[:, :ck]) * conv[:, ck:]).astype(_BF16)
    acc_ref[...] += _dot(act, wd_ref[...])

    @pl.when(j == pl.num_programs(1) - 1)
    def _():
        r = DN_ALPHA * h_ref[...] + acc_ref[...]
        out = _layer_norm(r, g_ref[...], b_ref[...])
        out_ref[...] = jnp.where(_real_rows(i, tm, t, tp), out, 0.0)


def _ffn(h, wu, cw, cb, wd, g, b, t, tp):
    n, d = h.shape
    tm = TM_FFN
    ck = FF_CHUNK
    halo = FFN_HALO
    hb = tm // halo
    last_halo = n // halo - 1
    return pl.pallas_call(
        functools.partial(_ffn_kernel, t=t, tp=tp),
        name="convffn",
        grid=(n // tm, D_FF // ck),
        in_specs=[pl.BlockSpec((tm, d), lambda i, j: (i, 0)),
                  pl.BlockSpec((halo, d), lambda i, j: (jnp.maximum(i * hb - 1, 0), 0)),
                  pl.BlockSpec((halo, d), lambda i, j: (jnp.minimum((i + 1) * hb, last_halo), 0)),
                  pl.BlockSpec((d, 2 * ck), lambda i, j: (0, j)),
                  pl.BlockSpec((FFN_CONV, 2 * ck), lambda i, j: (0, j)),
                  pl.BlockSpec((1, 2 * ck), lambda i, j: (0, j)),
                  pl.BlockSpec((ck, d), lambda i, j: (j, 0)),
                  _resident((1, d)), _resident((1, d))],
        out_specs=pl.BlockSpec((tm, d), lambda i, j: (i, 0)),
        out_shape=jax.ShapeDtypeStruct((n, d), _F32),
        scratch_shapes=[pltpu.VMEM((tm + 2 * halo, d), _BF16),
                        pltpu.VMEM((tm + 2 * halo, 2 * ck), _F32),
                        pltpu.VMEM((tm, d), _F32)],
        compiler_params=_params(("parallel", "arbitrary")),
    )(h, h, h, wu, cw, cb, wd, g, b)


def _swap_halves(w):
    half = w.shape[-1] // 2
    return jnp.concatenate([w[..., half:], w[..., :half]], axis=-1)


def _pair_chunks(w):
    lead = w.shape[:-1]
    w = w.reshape(lead + (2, D_FF // FF_CHUNK, FF_CHUNK))
    return jnp.swapaxes(w, -3, -2).reshape(lead + (2 * D_FF,))


def kernel(x, meta_tokens, ln0_g, ln0_b, w_in, q_norm, kv_norm, w_uq, w_uk, w_uv, w_o_mla, lru_conv_w, lru_conv_b, w_rg, b_rg, w_ig, b_ig, lru_lambda, w_o_lru, w_out, ln1_g, ln1_b, w_up, ffn_conv_w, ffn_conv_b, w_down, ln2_g, ln2_b):
    bsz, seq, d = x.shape
    t = N_META + seq
    tp = -(-t // LANES) * LANES
    n = bsz * tp
    assert d == D_MODEL and tp > t >= max(TM_LN0, TM_PROJ, TM_MIX, TM_FFN)
    assert n % TM_LN0 == 0 and n % TM_PROJ == 0 and n % TM_MIX == 0 and n % TM_FFN == 0
    row = lambda v: v.reshape(1, -1)

    meta = jnp.broadcast_to(meta_tokens[None].astype(x.dtype), (bsz, N_META, d))
    pad = jnp.zeros((bsz, tp - t, d), x.dtype)
    h = _ln0(jnp.concatenate([meta, x, pad], axis=1).reshape(n, d), row(ln0_g), row(ln0_b), t, tp)

    half = QK_ROPE // 2
    inv_freq = jnp.exp(-math.log(ROPE_THETA) * jnp.arange(half, dtype=_F32) / half)
    ang = jnp.arange(tp, dtype=_F32)[:, None] * inv_freq[None, :]
    cos, sin = jnp.cos(ang), jnp.sin(ang)
    cs = jnp.tile(jnp.concatenate([cos, cos, -sin, sin], axis=-1), (bsz, 1))
    cst = jnp.tile(jnp.concatenate([cos, sin], axis=-1).T, (1, bsz))

    for l in range(DEPTH):
        n_small = Q_RANK + KV_RANK + QK_ROPE
        ws = jnp.concatenate([w_in[l][:, :n_small], _swap_halves(w_in[l][:, Q_RANK + KV_RANK:n_small])],
                             axis=-1).astype(_BF16)
        wg = w_in[l][:, n_small:].astype(_BF16)
        wqt = w_uq[l].reshape(Q_RANK, MLA_HEADS * QK_DIM).T.astype(_BF16)
        wuk = w_uk[l].reshape(KV_RANK, MLA_HEADS * QK_NOPE).astype(_BF16)
        wuvt = w_uv[l].reshape(KV_RANK, MLA_HEADS * V_HEAD).T.astype(_BF16)

        qt, k, vt, gates = _proj(h, ws, wg, row(q_norm[l]), row(kv_norm[l]), wqt, wuk, wuvt, cs, cst)
        o = _attention(qt, k, vt, bsz, t, tp)

        w_gate = jnp.concatenate([w_rg[l][0], w_rg[l][1], w_ig[l][0], w_ig[l][1]], axis=-1).astype(_BF16)
        b_gate = jnp.concatenate([b_rg[l].reshape(2, LRU_BLOCKS, LRU_BLOCK_DIM),
                                  b_ig[l].reshape(2, LRU_BLOCKS, LRU_BLOCK_DIM)], axis=0)
        b_gate = jnp.transpose(b_gate, (1, 0, 2)).reshape(LRU_BLOCKS, 1, 4 * LRU_BLOCK_DIM)
        y = _lru(gates, lru_conv_w[l], row(lru_conv_b[l]), w_gate, b_gate, lru_lambda[l], bsz, t, tp)

        h = _mix(o, y, gates, h, w_o_mla[l].astype(_BF16), w_o_lru[l].astype(_BF16), w_out[l].astype(_BF16),
                 row(ln1_g[l]), row(ln1_b[l]), t, tp)
        h = _ffn(h, _pair_chunks(w_up[l]).astype(_BF16), _pair_chunks(ffn_conv_w[l]),
                 row(_pair_chunks(ffn_conv_b[l])), w_down[l].astype(_BF16), row(ln2_g[l]), row(ln2_b[l]), t, tp)

    return h.reshape(bsz, tp, d)[:, N_META:t]
```

```python
import functools
import math

import jax
import jax.numpy as jnp
from jax import lax
from jax.experimental import pallas as pl
from jax.experimental.pallas import tpu as pltpu

D_MODEL = 1024
N_META = 16
MLA_HEADS = 8
QK_NOPE = 128
QK_ROPE = 64
QK_DIM = QK_NOPE + QK_ROPE
V_HEAD = 128
Q_RANK = 256
KV_RANK = 128
ROPE_THETA = 10000.0
LRU_BLOCKS = 8
LRU_BLOCK_DIM = 128
LRU_CONV = 4
LRU_C = 8.0
D_FF = 2816
FFN_CONV = 3
DEPTH = 2
DN_ALPHA = (2.0 * DEPTH) ** 0.25
LN_EPS = 1e-5
RMS_EPS = 1e-6

SMALL_COLS = Q_RANK + KV_RANK + 2 * QK_ROPE
GATE_COLS = 4 * D_MODEL

LANES = 128
SUBLANES = 8
BF16_ROWS = 16
VMEM_LIMIT = 56 * 1024 * 1024

V_ROWS = V_HEAD + BF16_ROWS
TM_LN0 = 1024
TM_PROJ = 512
TM_MIX = 512
TM_FFN = 512
FF_CHUNK = 256
TQ = 512
TK = 256
LRU_SLABS = 2
CONV_HALO = SUBLANES
FFN_HALO = BF16_ROWS

_BF16 = jnp.bfloat16
_F32 = jnp.float32


def _params(sem):
    return pltpu.CompilerParams(dimension_semantics=sem, vmem_limit_bytes=VMEM_LIMIT)


def _resident(shape):
    zeros = (0,) * len(shape)
    return pl.BlockSpec(shape, lambda *_: zeros, pipeline_mode=pl.Buffered(1))


def _layer_norm(x, g, b):
    mu = jnp.mean(x, axis=-1, keepdims=True)
    xc = x - mu
    var = jnp.mean(xc * xc, axis=-1, keepdims=True)
    return xc * lax.rsqrt(var + LN_EPS) * g + b


def _rms_norm(x, g):
    ms = jnp.mean(x * x, axis=-1, keepdims=True)
    return x * lax.rsqrt(ms + RMS_EPS) * g


def _one_minus_sq(log_a, a):
    y = 2.0 * log_a
    u = a * a
    near_one = jnp.where(u == 1.0, y, (u - 1.0) * y * pl.reciprocal(jnp.log(u), approx=True))
    return -jnp.where(u > 0.5, near_one, u - 1.0)


def _sigmoid(x):
    return 0.5 * jnp.tanh(0.5 * x) + 0.5


def _sqrt(x):
    return jnp.where(x > 0.0, x * lax.rsqrt(x), 0.0)


def _dot(a, b):
    return jnp.dot(a, b, preferred_element_type=_F32)


def _dot_nt(a, b):
    return lax.dot_general(a, b, (((1,), (1,)), ((), ())), preferred_element_type=_F32)


def _real_rows(i, tm, t, tp):
    pos = lax.rem(i * tm, tp) + lax.broadcasted_iota(jnp.int32, (tm, 1), 0)
    return (pos < t) | (pos >= tp)


def _ln0_kernel(x_ref, g_ref, b_ref, o_ref, *, t, tp):
    tm = x_ref.shape[0]
    y = _layer_norm(x_ref[...], g_ref[...], b_ref[...])
    o_ref[...] = jnp.where(_real_rows(pl.program_id(0), tm, t, tp), y, 0.0)


def _ln0(x, g, b, t, tp):
    n, d = x.shape
    tm = TM_LN0
    return pl.pallas_call(
        functools.partial(_ln0_kernel, t=t, tp=tp),
        name="ln0",
        grid=(n // tm,),
        in_specs=[pl.BlockSpec((tm, d), lambda i: (i, 0)), _resident((1, d)), _resident((1, d))],
        out_specs=pl.BlockSpec((tm, d), lambda i: (i, 0)),
        out_shape=jax.ShapeDtypeStruct((n, d), _F32),
        compiler_params=_params(("parallel",)),
    )(x, g, b)


def _proj_kernel(h_ref, ws_ref, wg_ref, qn_ref, kvn_ref, wqt_ref, wuk_ref, wuvt_ref, cs_ref, cst_ref,
                 qt_ref, k_ref, vt_ref, g_ref):
    tm = h_ref.shape[0]
    hb = h_ref[...].astype(_BF16)
    small = _dot(hb, ws_ref[...])
    cq = _rms_norm(small[:, :Q_RANK], qn_ref[...]).astype(_BF16)
    ckv = _rms_norm(small[:, Q_RANK:Q_RANK + KV_RANK], kvn_ref[...]).astype(_BF16)

    prod = small[:, Q_RANK + KV_RANK:] * cs_ref[...]
    k_rope = (prod + pltpu.roll(prod, QK_ROPE, axis=1))[:, :QK_ROPE].astype(_BF16)
    k_nope = _dot(ckv, wuk_ref[...])
    for hd in range(MLA_HEADS):
        k_ref[hd, :, :QK_NOPE] = k_nope[:, hd * QK_NOPE:(hd + 1) * QK_NOPE].astype(_BF16)
        k_ref[hd, :, QK_NOPE:] = k_rope

    half = QK_ROPE // 2
    qt = _dot_nt(wqt_ref[...], cq) * (1.0 / math.sqrt(QK_DIM))
    cos_t = cst_ref[0:half, :]
    sin_t = cst_ref[half:QK_ROPE, :]
    for hd in range(MLA_HEADS):
        r0 = hd * QK_DIM
        x1 = qt[r0 + QK_NOPE:r0 + QK_NOPE + half]
        x2 = qt[r0 + QK_NOPE + half:r0 + QK_DIM]
        qt_ref[hd, 0:QK_NOPE, :] = qt[r0:r0 + QK_NOPE].astype(_BF16)
        qt_ref[hd, QK_NOPE:QK_NOPE + half, :] = (x1 * cos_t - x2 * sin_t).astype(_BF16)
        qt_ref[hd, QK_NOPE + half:QK_DIM, :] = (x2 * cos_t + x1 * sin_t).astype(_BF16)

    vt = _dot_nt(wuvt_ref[...], ckv)
    row = lax.broadcasted_iota(jnp.int32, (V_ROWS - V_HEAD, tm), 0)
    ones_rows = jnp.where(row == 0, 1.0, 0.0).astype(_BF16)
    for hd in range(MLA_HEADS):
        vt_ref[hd, 0:V_HEAD, :] = vt[hd * V_HEAD:(hd + 1) * V_HEAD].astype(_BF16)
        vt_ref[hd, V_HEAD:V_ROWS, :] = ones_rows

    for c in range(GATE_COLS // D_MODEL):
        cols = slice(c * D_MODEL, (c + 1) * D_MODEL)
        g_ref[:, cols] = _dot(hb, wg_ref[:, cols]).astype(_BF16)


def _proj(h, ws, wg, qn, kvn, wqt, wuk, wuvt, cs, cst):
    n, d = h.shape
    tm = TM_PROJ
    heads = lambda i: (0, i, 0)
    heads_t = lambda i: (0, 0, i)
    return pl.pallas_call(
        _proj_kernel,
        name="in_proj",
        grid=(n // tm,),
        in_specs=[pl.BlockSpec((tm, d), lambda i: (i, 0)),
                  _resident(ws.shape), _resident(wg.shape), _resident(qn.shape), _resident(kvn.shape),
                  _resident(wqt.shape), _resident(wuk.shape), _resident(wuvt.shape),
                  pl.BlockSpec((tm, 2 * QK_ROPE), lambda i: (i, 0)),
                  pl.BlockSpec((QK_ROPE, tm), lambda i: (0, i))],
        out_specs=[pl.BlockSpec((MLA_HEADS, QK_DIM, tm), heads_t),
                   pl.BlockSpec((MLA_HEADS, tm, QK_DIM), heads),
                   pl.BlockSpec((MLA_HEADS, V_ROWS, tm), heads_t),
                   pl.BlockSpec((tm, GATE_COLS), lambda i: (i, 0))],
        out_shape=[jax.ShapeDtypeStruct((MLA_HEADS, QK_DIM, n), _BF16),
                   jax.ShapeDtypeStruct((MLA_HEADS, n, QK_DIM), _BF16),
                   jax.ShapeDtypeStruct((MLA_HEADS, V_ROWS, n), _BF16),
                   jax.ShapeDtypeStruct((n, GATE_COLS), _BF16)],
        compiler_params=_params(("parallel",)),
    )(h, ws, wg, qn, kvn, wqt, wuk, wuvt, cs, cst)


def _attn_kernel(qt_ref, k_ref, vt_ref, o_ref, s0_ref, s1_ref, *, t, tp):
    n_chunks = tp // TK
    tail_k = tp - n_chunks * TK
    n_tiles = tp // TQ
    tail_q = tp - n_tiles * TQ

    def scores(start, tq, s_ref):
        s_ref[:, 0:tq] = _dot(k_ref[...], qt_ref[:, pl.ds(start, tq)])
        s_ref[t:tp, 0:tq] = jnp.full((tp - t, tq), -jnp.inf, _F32)
        return jnp.max(s_ref[:, 0:tq], axis=0, keepdims=True)

    def values(start, tq, s_ref, m):
        acc = jnp.zeros((V_ROWS, tq), _F32)
        for r0, rows in [(c * TK, TK) for c in range(n_chunks)] + [(n_chunks * TK, tail_k)]:
            p = jnp.exp(s_ref[r0:r0 + rows, 0:tq] - m).astype(_BF16)
            acc = acc + _dot(vt_ref[:, r0:r0 + rows], p)
        o_t = acc[:V_HEAD] / acc[V_HEAD:V_HEAD + 1]
        o_ref[pl.ds(start, tq), :] = o_t.T.astype(o_ref.dtype)

    def tile(i):
        return pl.multiple_of(i * TQ, TQ)

    m0 = scores(0, TQ, s0_ref)

    def body(j, m0):
        m1 = scores(tile(2 * j + 1), TQ, s1_ref)
        values(tile(2 * j), TQ, s0_ref, m0)
        m0 = scores(tile(2 * j + 2), TQ, s0_ref)
        values(tile(2 * j + 1), TQ, s1_ref, m1)
        return m0

    m0 = lax.fori_loop(0, n_tiles // 2 - 1, body, m0)
    m1 = scores((n_tiles - 1) * TQ, TQ, s1_ref)
    values((n_tiles - 2) * TQ, TQ, s0_ref, m0)
    m0 = scores(n_tiles * TQ, tail_q, s0_ref)
    values((n_tiles - 1) * TQ, TQ, s1_ref, m1)
    values(n_tiles * TQ, tail_q, s0_ref, m0)


def _attention(qt, k, vt, bsz, t, tp):
    n = k.shape[1]
    assert (tp // TQ) % 2 == 0 and 0 < tp % TQ <= TQ and tp % TQ % LANES == 0 and tp % TK % BF16_ROWS == 0
    return pl.pallas_call(
        functools.partial(_attn_kernel, t=t, tp=tp),
        name="mla_attention",
        grid=(bsz, MLA_HEADS),
        in_specs=[pl.BlockSpec((None, QK_DIM, tp), lambda b, hd: (hd, 0, b)),
                  pl.BlockSpec((None, tp, QK_DIM), lambda b, hd: (hd, b, 0)),
                  pl.BlockSpec((None, V_ROWS, tp), lambda b, hd: (hd, 0, b))],
        out_specs=pl.BlockSpec((tp, V_HEAD), lambda b, hd: (b, hd)),
        out_shape=jax.ShapeDtypeStruct((n, MLA_HEADS * V_HEAD), _BF16),
        scratch_shapes=[pltpu.VMEM((tp, TQ), _F32)] * 2,
        compiler_params=_params(("parallel", "parallel")),
    )(qt, k, vt)


def _lru_kernel(g_ref, x_ref, cw_ref, cb_ref, wg_ref, bg_ref, lam_ref, y_ref,
                xp_ref, af_ref, uf_ref, ab_ref, ub_ref, *, t, tp):
    bd = LRU_BLOCK_DIM
    n_chunk = SUBLANES
    clen = t // n_chunk
    lam = lam_ref[...]
    sp = jnp.maximum(-lam, 0.0) + jnp.log1p(jnp.exp(-jnp.abs(lam)))

    for s in range(LRU_SLABS):
        lanes = slice(s * bd, (s + 1) * bd)
        xp_ref[0:CONV_HALO, :] = jnp.zeros((CONV_HALO, bd), _F32)
        xp_ref[CONV_HALO:CONV_HALO + tp, :] = x_ref[:, lanes].astype(_F32)
        xc = cb_ref[:, lanes]
        for tap in range(LRU_CONV):
            off = CONV_HALO + tap - LRU_CONV // 2
            xc = xc + xp_ref[off:off + t, :] * cw_ref[tap:tap + 1, lanes]

        gates = _sigmoid(_dot(xc.astype(_BF16), wg_ref[s]) + bg_ref[s])
        for d, (a_ref, u_ref) in enumerate(((af_ref, uf_ref), (ab_ref, ub_ref))):
            r = gates[:, d * bd:(d + 1) * bd]
            ig = gates[:, (2 + d) * bd:(3 + d) * bd]
            log_a = -LRU_C * r * sp[d:d + 1, lanes]
            a = jnp.exp(log_a)
            a_ref[s] = a
            u_ref[s] = _sqrt(_one_minus_sq(log_a, a)) * (ig * xc)

    def rows(j):
        return pl.ds(j, n_chunk, stride=clen)

    def pass1(j, carry):
        jb = clen - 1 - j
        out = []
        for s in range(LRU_SLABS):
            hf, pf, hb, pb = carry[4 * s:4 * s + 4]
            a = af_ref[s, rows(j), :]
            hf = a * hf + uf_ref[s, rows(j), :]
            pf = pf * a
            uf_ref[s, rows(j), :] = hf
            af_ref[s, rows(j), :] = pf
            a = ab_ref[s, rows(jb), :]
            hb = a * hb + ub_ref[s, rows(jb), :]
            pb = pb * a
            ub_ref[s, rows(jb), :] = hb
            ab_ref[s, rows(jb), :] = pb
            out += [hf, pf, hb, pb]
        return tuple(out)

    zero = jnp.zeros((n_chunk, bd), _F32)
    one = jnp.ones((n_chunk, bd), _F32)
    ends = lax.fori_loop(0, clen, pass1, (zero, one, zero, one) * LRU_SLABS)

    carries = []
    for s in range(LRU_SLABS):
        hf, pf, hb, pb = ends[4 * s:4 * s + 4]
        cf = [jnp.zeros((1, bd), _F32)]
        for c in range(1, n_chunk):
            cf.append(hf[c - 1:c, :] + pf[c - 1:c, :] * cf[c - 1])
        cb = [jnp.zeros((1, bd), _F32)]
        for c in range(n_chunk - 2, -1, -1):
            cb.insert(0, hb[c + 1:c + 2, :] + pb[c + 1:c + 2, :] * cb[0])
        carries.append((jnp.concatenate(cf, axis=0), jnp.concatenate(cb, axis=0)))

    def pass2(j, carry):
        for s in range(LRU_SLABS):
            cf, cb = carries[s]
            uf_ref[s, rows(j), :] = (uf_ref[s, rows(j), :] + af_ref[s, rows(j), :] * cf
                                     + ub_ref[s, rows(j), :] + ab_ref[s, rows(j), :] * cb)
        return carry

    lax.fori_loop(0, clen, pass2, 0)
    for s in range(LRU_SLABS):
        lanes = slice(s * bd, (s + 1) * bd)
        y_ref[0:t, lanes] = (jax.nn.gelu(g_ref[0:t, lanes].astype(_F32)) * uf_ref[s]).astype(y_ref.dtype)
    y_ref[t:tp, :] = jnp.zeros((tp - t, LRU_SLABS * bd), y_ref.dtype)


def _lru(gates, cw, cb, wg, bg, lam, bsz, t, tp):
    n = gates.shape[0]
    bd = LRU_BLOCK_DIM
    w = LRU_SLABS * bd
    nb = D_MODEL // w
    assert t % SUBLANES == 0
    return pl.pallas_call(
        functools.partial(_lru_kernel, t=t, tp=tp),
        name="rg_lru",
        grid=(bsz, nb),
        in_specs=[pl.BlockSpec((tp, w), lambda b, g: (b, g)),
                  pl.BlockSpec((tp, w), lambda b, g: (b, nb + g)),
                  pl.BlockSpec((LRU_CONV, w), lambda b, g: (0, g)),
                  pl.BlockSpec((1, w), lambda b, g: (0, g)),
                  pl.BlockSpec((LRU_SLABS, bd, 4 * bd), lambda b, g: (g, 0, 0)),
                  pl.BlockSpec((LRU_SLABS, 1, 4 * bd), lambda b, g: (g, 0, 0)),
                  pl.BlockSpec((2, w), lambda b, g: (0, g))],
        out_specs=pl.BlockSpec((tp, w), lambda b, g: (b, g)),
        out_shape=jax.ShapeDtypeStruct((n, D_MODEL), _BF16),
        scratch_shapes=[pltpu.VMEM((tp + CONV_HALO, bd), _F32)] + [pltpu.VMEM((LRU_SLABS, t, bd), _F32)] * 4,
        compiler_params=_params(("parallel", "parallel")),
    )(gates, gates, cw, cb, wg, bg, lam)


def _mix_kernel(o_ref, y_ref, gm_ref, gl_ref, h_ref, wom_ref, wol_ref, wout_ref, g_ref, b_ref, out_ref, *, t, tp):
    tm = h_ref.shape[0]
    y_mla = _dot(o_ref[...], wom_ref[...])
    y_lru = _dot(y_ref[...], wol_ref[...])
    z = (jax.nn.sigmoid(gm_ref[...].astype(_F32)) * y_mla
         + jax.nn.sigmoid(gl_ref[...].astype(_F32)) * y_lru)
    r = DN_ALPHA * h_ref[...] + _dot(z.astype(_BF16), wout_ref[...])
    out = _layer_norm(r, g_ref[...], b_ref[...])
    out_ref[...] = jnp.where(_real_rows(pl.program_id(0), tm, t, tp), out, 0.0)


def _mix(o, y, gates, h, wom, wol, wout, g, b, t, tp):
    n, d = h.shape
    tm = TM_MIX
    tile = lambda i: (i, 0)
    return pl.pallas_call(
        functools.partial(_mix_kernel, t=t, tp=tp),
        name="mixer_out",
        grid=(n // tm,),
        in_specs=[pl.BlockSpec((tm, d), tile),
                  pl.BlockSpec((tm, d), tile),
                  pl.BlockSpec((tm, d), lambda i: (i, 2)),
                  pl.BlockSpec((tm, d), lambda i: (i, 3)),
                  pl.BlockSpec((tm, d), tile),
                  _resident((d, d)), _resident((d, d)), _resident((d, d)),
                  _resident((1, d)), _resident((1, d))],
        out_specs=pl.BlockSpec((tm, d), tile),
        out_shape=jax.ShapeDtypeStruct((n, d), _F32),
        compiler_params=_params(("parallel",)),
    )(o, y, gates, gates, h, wom, wol, wout, g, b)


def _ffn_kernel(h_ref, hp_ref, hn_ref, wu_ref, cw_ref, cb_ref, wd_ref, g_ref, b_ref, out_ref,
                lhs_ref, up0_ref, up1_ref, act_ref, *, t, tp):
    tm = h_ref.shape[0]
    halo = FFN_HALO
    nj = act_ref.shape[0]
    n_slab = up0_ref.shape[0] // 2
    i = pl.program_id(0)

    lhs_ref[0:halo, :] = hp_ref[...].astype(_BF16)
    lhs_ref[halo:halo + tm, :] = h_ref[...].astype(_BF16)
    lhs_ref[halo + tm:, :] = hn_ref[...].astype(_BF16)

    @pl.when(i == 0)
    def _():
        lhs_ref[0:halo, :] = jnp.zeros((halo, lhs_ref.shape[1]), _BF16)

    def up(j, up_ref):
        lhs = lhs_ref[...]
        for half in range(2):
            res = _dot(lhs, wu_ref[half * nj + j])
            for s in range(n_slab):
                up_ref[half * n_slab + s] = res[:, s * LANES:(s + 1) * LANES]

    def conv(up_ref, slab, chunk, s):
        lanes = slice(s * LANES, (s + 1) * LANES)
        cw = cw_ref[chunk]
        out = cb_ref[chunk][:, lanes]
        for tap in range(FFN_CONV):
            off = halo + tap - FFN_CONV // 2
            out = out + up_ref[slab, off:off + tm, :] * cw[tap:tap + 1, lanes]
        return out

    def act(j, up_ref):
        for s in range(n_slab):
            gate = conv(up_ref, s, j, s)
            val = conv(up_ref, n_slab + s, nj + j, s)
            act_ref[j, :, s * LANES:(s + 1) * LANES] = (jax.nn.gelu(gate) * val).astype(_BF16)

    bufs = (up0_ref, up1_ref)
    up(0, bufs[0])
    for j in range(nj):
        if j + 1 < nj:
            up(j + 1, bufs[(j + 1) % 2])
        act(j, bufs[j % 2])

    f = _dot(act_ref[0], wd_ref[0])
    for j in range(1, nj):
        f = f + _dot(act_ref[j], wd_ref[j])
    out = _layer_norm(DN_ALPHA * h_ref[...] + f, g_ref[...], b_ref[...])
    out_ref[...] = jnp.where(_real_rows(i, tm, t, tp), out, 0.0)


def _ffn(h, wu, cw, cb, wd, g, b, t, tp):
    n, d = h.shape
    tm = TM_FFN
    ck = FF_CHUNK
    nj = wd.shape[0]
    halo = FFN_HALO
    hb = tm // halo
    last_halo = n // halo - 1
    up_scratch = pltpu.VMEM((2 * ck // LANES, tm + 2 * halo, LANES), _F32)
    return pl.pallas_call(
        functools.partial(_ffn_kernel, t=t, tp=tp),
        name="convffn",
        grid=(n // tm,),
        in_specs=[pl.BlockSpec((tm, d), lambda i: (i, 0)),
                  pl.BlockSpec((halo, d), lambda i: (jnp.maximum(i * hb - 1, 0), 0)),
                  pl.BlockSpec((halo, d), lambda i: (jnp.minimum((i + 1) * hb, last_halo), 0)),
                  _resident(wu.shape), _resident(cw.shape), _resident(cb.shape), _resident(wd.shape),
                  _resident((1, d)), _resident((1, d))],
        out_specs=pl.BlockSpec((tm, d), lambda i: (i, 0)),
        out_shape=jax.ShapeDtypeStruct((n, d), _F32),
        scratch_shapes=[pltpu.VMEM((tm + 2 * halo, d), _BF16), up_scratch, up_scratch,
                        pltpu.VMEM((nj, tm, ck), _BF16)],
        compiler_params=_params(("parallel",)),
    )(h, h, h, wu, cw, cb, wd, g, b)


def _swap_halves(w):
    half = w.shape[-1] // 2
    return jnp.concatenate([w[..., half:], w[..., :half]], axis=-1)


def _chunk_major(w):
    rows = w.shape[0]
    return jnp.transpose(w.reshape(rows, 2 * D_FF // FF_CHUNK, FF_CHUNK), (1, 0, 2))


def kernel(x, meta_tokens, ln0_g, ln0_b, w_in, q_norm, kv_norm, w_uq, w_uk, w_uv, w_o_mla, lru_conv_w, lru_conv_b, w_rg, b_rg, w_ig, b_ig, lru_lambda, w_o_lru, w_out, ln1_g, ln1_b, w_up, ffn_conv_w, ffn_conv_b, w_down, ln2_g, ln2_b):
    bsz, seq, d = x.shape
    t = N_META + seq
    tp = -(-t // LANES) * LANES
    n = bsz * tp
    assert d == D_MODEL and tp > t >= max(TM_LN0, TM_PROJ, TM_MIX, TM_FFN)
    assert n % TM_LN0 == 0 and n % TM_PROJ == 0 and n % TM_MIX == 0 and n % TM_FFN == 0
    row = lambda v: v.reshape(1, -1)

    meta = jnp.broadcast_to(meta_tokens[None].astype(x.dtype), (bsz, N_META, d))
    pad = jnp.zeros((bsz, tp - t, d), x.dtype)
    h = _ln0(jnp.concatenate([meta, x, pad], axis=1).reshape(n, d), row(ln0_g), row(ln0_b), t, tp)

    half = QK_ROPE // 2
    inv_freq = jnp.exp(-math.log(ROPE_THETA) * jnp.arange(half, dtype=_F32) / half)
    ang = jnp.arange(tp, dtype=_F32)[:, None] * inv_freq[None, :]
    cos, sin = jnp.cos(ang), jnp.sin(ang)
    cs = jnp.tile(jnp.concatenate([cos, cos, -sin, sin], axis=-1), (bsz, 1))
    cst = jnp.tile(jnp.concatenate([cos, sin], axis=-1).T, (1, bsz))

    for l in range(DEPTH):
        n_small = Q_RANK + KV_RANK + QK_ROPE
        ws = jnp.concatenate([w_in[l][:, :n_small], _swap_halves(w_in[l][:, Q_RANK + KV_RANK:n_small])],
                             axis=-1).astype(_BF16)
        wg = w_in[l][:, n_small:].astype(_BF16)
        wqt = w_uq[l].reshape(Q_RANK, MLA_HEADS * QK_DIM).T.astype(_BF16)
        wuk = w_uk[l].reshape(KV_RANK, MLA_HEADS * QK_NOPE).astype(_BF16)
        wuvt = w_uv[l].reshape(KV_RANK, MLA_HEADS * V_HEAD).T.astype(_BF16)

        qt, k, vt, gates = _proj(h, ws, wg, row(q_norm[l]), row(kv_norm[l]), wqt, wuk, wuvt, cs, cst)
        o = _attention(qt, k, vt, bsz, t, tp)

        w_gate = jnp.concatenate([w_rg[l][0], w_rg[l][1], w_ig[l][0], w_ig[l][1]], axis=-1).astype(_BF16)
        b_gate = jnp.concatenate([b_rg[l].reshape(2, LRU_BLOCKS, LRU_BLOCK_DIM),
                                  b_ig[l].reshape(2, LRU_BLOCKS, LRU_BLOCK_DIM)], axis=0)
        b_gate = jnp.transpose(b_gate, (1, 0, 2)).reshape(LRU_BLOCKS, 1, 4 * LRU_BLOCK_DIM)
        y = _lru(gates, lru_conv_w[l], row(lru_conv_b[l]), w_gate, b_gate, lru_lambda[l], bsz, t, tp)

        h = _mix(o, y, gates, h, w_o_mla[l].astype(_BF16), w_o_lru[l].astype(_BF16), w_out[l].astype(_BF16),
                 row(ln1_g[l]), row(ln1_b[l]), t, tp)
        h = _ffn(h, _chunk_major(w_up[l].astype(_BF16)), _chunk_major(ffn_conv_w[l]),
                 _chunk_major(row(ffn_conv_b[l])), w_down[l].astype(_BF16).reshape(D_FF // FF_CHUNK, FF_CHUNK, d),
                 row(ln2_g[l]), row(ln2_b[l]), t, tp)

    return h.reshape(bsz, tp, d)[:, N_META:t]
```

```python
import functools
import math

import jax
import jax.numpy as jnp
from jax import lax
from jax.experimental import pallas as pl
from jax.experimental.pallas import tpu as pltpu

D_MODEL = 1024
N_META = 16
MLA_HEADS = 8
QK_NOPE = 128
QK_ROPE = 64
QK_DIM = QK_NOPE + QK_ROPE
V_HEAD = 128
Q_RANK = 256
KV_RANK = 128
ROPE_THETA = 10000.0
LRU_BLOCKS = 8
LRU_BLOCK_DIM = 128
LRU_CONV = 4
LRU_C = 8.0
D_FF = 2816
FFN_CONV = 3
DEPTH = 2
DN_ALPHA = (2.0 * DEPTH) ** 0.25
LN_EPS = 1e-5
RMS_EPS = 1e-6

SMALL_COLS = Q_RANK + KV_RANK + 2 * QK_ROPE
GATE_COLS = 4 * D_MODEL

LANES = 128
SUBLANES = 8
BF16_ROWS = 16
VMEM_LIMIT = 56 * 1024 * 1024

V_ROWS = V_HEAD + BF16_ROWS
TM_LN0 = 1024
TM_PROJ = 512
TM_MIX = 512
TM_FFN = 512
FF_CHUNK = 256
TQ = 512
TK = 256
LRU_SLABS = 2
SCAN_CHUNKS = 2 * SUBLANES
CONV_HALO = SUBLANES
FFN_HALO = BF16_ROWS

_BF16 = jnp.bfloat16
_F32 = jnp.float32


def _params(sem):
    return pltpu.CompilerParams(dimension_semantics=sem, vmem_limit_bytes=VMEM_LIMIT)


def _resident(shape):
    zeros = (0,) * len(shape)
    return pl.BlockSpec(shape, lambda *_: zeros, pipeline_mode=pl.Buffered(1))


def _layer_norm(x, g, b):
    mu = jnp.mean(x, axis=-1, keepdims=True)
    xc = x - mu
    var = jnp.mean(xc * xc, axis=-1, keepdims=True)
    return xc * lax.rsqrt(var + LN_EPS) * g + b


def _rms_norm(x, g):
    ms = jnp.mean(x * x, axis=-1, keepdims=True)
    return x * lax.rsqrt(ms + RMS_EPS) * g


def _one_minus_sq(log_a, a):
    y = 2.0 * log_a
    u = a * a
    near_one = jnp.where(u == 1.0, y, (u - 1.0) * y * pl.reciprocal(jnp.log(u), approx=True))
    return -jnp.where(u > 0.5, near_one, u - 1.0)


def _sigmoid(x):
    return 0.5 * jnp.tanh(0.5 * x) + 0.5


def _sqrt(x):
    return jnp.where(x > 0.0, x * lax.rsqrt(x), 0.0)


def _dot(a, b):
    return jnp.dot(a, b, preferred_element_type=_F32)


def _dot_nt(a, b):
    return lax.dot_general(a, b, (((1,), (1,)), ((), ())), preferred_element_type=_F32)


def _real_rows(i, tm, t, tp):
    pos = lax.rem(i * tm, tp) + lax.broadcasted_iota(jnp.int32, (tm, 1), 0)
    return (pos < t) | (pos >= tp)


def _ln0_kernel(x_ref, g_ref, b_ref, o_ref, *, t, tp):
    tm = x_ref.shape[0]
    y = _layer_norm(x_ref[...], g_ref[...], b_ref[...])
    o_ref[...] = jnp.where(_real_rows(pl.program_id(0), tm, t, tp), y, 0.0)


def _ln0(x, g, b, t, tp):
    n, d = x.shape
    tm = TM_LN0
    return pl.pallas_call(
        functools.partial(_ln0_kernel, t=t, tp=tp),
        name="ln0",
        grid=(n // tm,),
        in_specs=[pl.BlockSpec((tm, d), lambda i: (i, 0)), _resident((1, d)), _resident((1, d))],
        out_specs=pl.BlockSpec((tm, d), lambda i: (i, 0)),
        out_shape=jax.ShapeDtypeStruct((n, d), _F32),
        compiler_params=_params(("parallel",)),
    )(x, g, b)


def _proj_kernel(h_ref, ws_ref, wg_ref, qn_ref, kvn_ref, wqt_ref, wuk_ref, wuvt_ref, cs_ref, cst_ref,
                 qt_ref, k_ref, vt_ref, g_ref):
    tm = h_ref.shape[0]
    hb = h_ref[...].astype(_BF16)
    small = _dot(hb, ws_ref[...])
    cq = _rms_norm(small[:, :Q_RANK], qn_ref[...]).astype(_BF16)
    ckv = _rms_norm(small[:, Q_RANK:Q_RANK + KV_RANK], kvn_ref[...]).astype(_BF16)

    prod = small[:, Q_RANK + KV_RANK:] * cs_ref[...]
    k_rope = (prod + pltpu.roll(prod, QK_ROPE, axis=1))[:, :QK_ROPE].astype(_BF16)
    k_nope = _dot(ckv, wuk_ref[...])
    for hd in range(MLA_HEADS):
        k_ref[hd, :, :QK_NOPE] = k_nope[:, hd * QK_NOPE:(hd + 1) * QK_NOPE].astype(_BF16)
        k_ref[hd, :, QK_NOPE:] = k_rope

    half = QK_ROPE // 2
    qt = _dot_nt(wqt_ref[...], cq) * (1.0 / math.sqrt(QK_DIM))
    cos_t = cst_ref[0:half, :]
    sin_t = cst_ref[half:QK_ROPE, :]
    for hd in range(MLA_HEADS):
        r0 = hd * QK_DIM
        x1 = qt[r0 + QK_NOPE:r0 + QK_NOPE + half]
        x2 = qt[r0 + QK_NOPE + half:r0 + QK_DIM]
        qt_ref[hd, 0:QK_NOPE, :] = qt[r0:r0 + QK_NOPE].astype(_BF16)
        qt_ref[hd, QK_NOPE:QK_NOPE + half, :] = (x1 * cos_t - x2 * sin_t).astype(_BF16)
        qt_ref[hd, QK_NOPE + half:QK_DIM, :] = (x2 * cos_t + x1 * sin_t).astype(_BF16)

    vt = _dot_nt(wuvt_ref[...], ckv)
    row = lax.broadcasted_iota(jnp.int32, (V_ROWS - V_HEAD, tm), 0)
    ones_rows = jnp.where(row == 0, 1.0, 0.0).astype(_BF16)
    for hd in range(MLA_HEADS):
        vt_ref[hd, 0:V_HEAD, :] = vt[hd * V_HEAD:(hd + 1) * V_HEAD].astype(_BF16)
        vt_ref[hd, V_HEAD:V_ROWS, :] = ones_rows

    for c in range(GATE_COLS // D_MODEL):
        cols = slice(c * D_MODEL, (c + 1) * D_MODEL)
        g_ref[:, cols] = _dot(hb, wg_ref[:, cols]).astype(_BF16)


def _proj(h, ws, wg, qn, kvn, wqt, wuk, wuvt, cs, cst):
    n, d = h.shape
    tm = TM_PROJ
    heads = lambda i: (0, i, 0)
    heads_t = lambda i: (0, 0, i)
    return pl.pallas_call(
        _proj_kernel,
        name="in_proj",
        grid=(n // tm,),
        in_specs=[pl.BlockSpec((tm, d), lambda i: (i, 0)),
                  _resident(ws.shape), _resident(wg.shape), _resident(qn.shape), _resident(kvn.shape),
                  _resident(wqt.shape), _resident(wuk.shape), _resident(wuvt.shape),
                  pl.BlockSpec((tm, 2 * QK_ROPE), lambda i: (i, 0)),
                  pl.BlockSpec((QK_ROPE, tm), lambda i: (0, i))],
        out_specs=[pl.BlockSpec((MLA_HEADS, QK_DIM, tm), heads_t),
                   pl.BlockSpec((MLA_HEADS, tm, QK_DIM), heads),
                   pl.BlockSpec((MLA_HEADS, V_ROWS, tm), heads_t),
                   pl.BlockSpec((tm, GATE_COLS), lambda i: (i, 0))],
        out_shape=[jax.ShapeDtypeStruct((MLA_HEADS, QK_DIM, n), _BF16),
                   jax.ShapeDtypeStruct((MLA_HEADS, n, QK_DIM), _BF16),
                   jax.ShapeDtypeStruct((MLA_HEADS, V_ROWS, n), _BF16),
                   jax.ShapeDtypeStruct((n, GATE_COLS), _BF16)],
        compiler_params=_params(("parallel",)),
    )(h, ws, wg, qn, kvn, wqt, wuk, wuvt, cs, cst)


def _attn_kernel(qt_ref, k_ref, vt_ref, o_ref, s0_ref, s1_ref, *, t, tp):
    n_chunks = tp // TK
    tail_k = tp - n_chunks * TK
    n_tiles = tp // TQ
    tail_q = tp - n_tiles * TQ

    def scores(start, tq, s_ref):
        s_ref[:, 0:tq] = _dot(k_ref[...], qt_ref[:, pl.ds(start, tq)])
        s_ref[t:tp, 0:tq] = jnp.full((tp - t, tq), -jnp.inf, _F32)
        return jnp.max(s_ref[:, 0:tq], axis=0, keepdims=True)

    def values(start, tq, s_ref, m):
        acc = jnp.zeros((V_ROWS, tq), _F32)
        for r0, rows in [(c * TK, TK) for c in range(n_chunks)] + [(n_chunks * TK, tail_k)]:
            p = jnp.exp(s_ref[r0:r0 + rows, 0:tq] - m).astype(_BF16)
            acc = acc + _dot(vt_ref[:, r0:r0 + rows], p)
        o_t = acc[:V_HEAD] / acc[V_HEAD:V_HEAD + 1]
        o_ref[pl.ds(start, tq), :] = o_t.T.astype(o_ref.dtype)

    def tile(i):
        return pl.multiple_of(i * TQ, TQ)

    m0 = scores(0, TQ, s0_ref)

    def body(j, m0):
        m1 = scores(tile(2 * j + 1), TQ, s1_ref)
        values(tile(2 * j), TQ, s0_ref, m0)
        m0 = scores(tile(2 * j + 2), TQ, s0_ref)
        values(tile(2 * j + 1), TQ, s1_ref, m1)
        return m0

    m0 = lax.fori_loop(0, n_tiles // 2 - 1, body, m0)
    m1 = scores((n_tiles - 1) * TQ, TQ, s1_ref)
    values((n_tiles - 2) * TQ, TQ, s0_ref, m0)
    m0 = scores(n_tiles * TQ, tail_q, s0_ref)
    values((n_tiles - 1) * TQ, TQ, s1_ref, m1)
    values(n_tiles * TQ, tail_q, s0_ref, m0)


def _attention(qt, k, vt, bsz, t, tp):
    n = k.shape[1]
    assert (tp // TQ) % 2 == 0 and 0 < tp % TQ <= TQ and tp % TQ % LANES == 0 and tp % TK % BF16_ROWS == 0
    return pl.pallas_call(
        functools.partial(_attn_kernel, t=t, tp=tp),
        name="mla_attention",
        grid=(bsz, MLA_HEADS),
        in_specs=[pl.BlockSpec((None, QK_DIM, tp), lambda b, hd: (hd, 0, b)),
                  pl.BlockSpec((None, tp, QK_DIM), lambda b, hd: (hd, b, 0)),
                  pl.BlockSpec((None, V_ROWS, tp), lambda b, hd: (hd, 0, b))],
        out_specs=pl.BlockSpec((tp, V_HEAD), lambda b, hd: (b, hd)),
        out_shape=jax.ShapeDtypeStruct((n, MLA_HEADS * V_HEAD), _BF16),
        scratch_shapes=[pltpu.VMEM((tp, TQ), _F32)] * 2,
        compiler_params=_params(("parallel", "parallel")),
    )(qt, k, vt)


def _lru_kernel(g_ref, x_ref, cw_ref, cb_ref, wg_ref, bg_ref, lam_ref, y_ref,
                xp_ref, af_ref, uf_ref, ab_ref, ub_ref, *, t, tp):
    bd = LRU_BLOCK_DIM
    n_chunk = SCAN_CHUNKS
    clen = t // n_chunk
    lam = lam_ref[...]
    sp = jnp.maximum(-lam, 0.0) + jnp.log1p(jnp.exp(-jnp.abs(lam)))

    for s in range(LRU_SLABS):
        lanes = slice(s * bd, (s + 1) * bd)
        xp_ref[0:CONV_HALO, :] = jnp.zeros((CONV_HALO, bd), _F32)
        xp_ref[CONV_HALO:CONV_HALO + tp, :] = x_ref[:, lanes].astype(_F32)
        xc = cb_ref[:, lanes]
        for tap in range(LRU_CONV):
            off = CONV_HALO + tap - LRU_CONV // 2
            xc = xc + xp_ref[off:off + t, :] * cw_ref[tap:tap + 1, lanes]

        gates = _sigmoid(_dot(xc.astype(_BF16), wg_ref[s]) + bg_ref[s])
        for d, (a_ref, u_ref) in enumerate(((af_ref, uf_ref), (ab_ref, ub_ref))):
            r = gates[:, d * bd:(d + 1) * bd]
            ig = gates[:, (2 + d) * bd:(3 + d) * bd]
            log_a = -LRU_C * r * sp[d:d + 1, lanes]
            a = jnp.exp(log_a)
            a_ref[s] = a
            u_ref[s] = _sqrt(_one_minus_sq(log_a, a)) * (ig * xc)

    def rows(j):
        return pl.ds(j, n_chunk, stride=clen)

    def pass1(j, carry):
        jb = clen - 1 - j
        out = []
        for s in range(LRU_SLABS):
            hf, pf, hb, pb = carry[4 * s:4 * s + 4]
            a = af_ref[s, rows(j), :]
            hf = a * hf + uf_ref[s, rows(j), :]
            pf = pf * a
            uf_ref[s, rows(j), :] = hf
            af_ref[s, rows(j), :] = pf
            a = ab_ref[s, rows(jb), :]
            hb = a * hb + ub_ref[s, rows(jb), :]
            pb = pb * a
            ub_ref[s, rows(jb), :] = hb
            ab_ref[s, rows(jb), :] = pb
            out += [hf, pf, hb, pb]
        return tuple(out)

    zero = jnp.zeros((n_chunk, bd), _F32)
    one = jnp.ones((n_chunk, bd), _F32)
    ends = lax.fori_loop(0, clen, pass1, (zero, one, zero, one) * LRU_SLABS)

    carries = []
    for s in range(LRU_SLABS):
        hf, pf, hb, pb = ends[4 * s:4 * s + 4]
        cf = [jnp.zeros((1, bd), _F32)]
        for c in range(1, n_chunk):
            cf.append(hf[c - 1:c, :] + pf[c - 1:c, :] * cf[c - 1])
        cb = [jnp.zeros((1, bd), _F32)]
        for c in range(n_chunk - 2, -1, -1):
            cb.insert(0, hb[c + 1:c + 2, :] + pb[c + 1:c + 2, :] * cb[0])
        carries.append((jnp.concatenate(cf, axis=0), jnp.concatenate(cb, axis=0)))

    def pass2(j, carry):
        for s in range(LRU_SLABS):
            cf, cb = carries[s]
            uf_ref[s, rows(j), :] = (uf_ref[s, rows(j), :] + af_ref[s, rows(j), :] * cf
                                     + ub_ref[s, rows(j), :] + ab_ref[s, rows(j), :] * cb)
        return carry

    lax.fori_loop(0, clen, pass2, 0, unroll=2)
    for s in range(LRU_SLABS):
        lanes = slice(s * bd, (s + 1) * bd)
        y_ref[0:t, lanes] = (jax.nn.gelu(g_ref[0:t, lanes].astype(_F32)) * uf_ref[s]).astype(y_ref.dtype)
    y_ref[t:tp, :] = jnp.zeros((tp - t, LRU_SLABS * bd), y_ref.dtype)


def _lru(gates, cw, cb, wg, bg, lam, bsz, t, tp):
    n = gates.shape[0]
    bd = LRU_BLOCK_DIM
    w = LRU_SLABS * bd
    nb = D_MODEL // w
    assert t % SCAN_CHUNKS == 0
    return pl.pallas_call(
        functools.partial(_lru_kernel, t=t, tp=tp),
        name="rg_lru",
        grid=(bsz, nb),
        in_specs=[pl.BlockSpec((tp, w), lambda b, g: (b, g)),
                  pl.BlockSpec((tp, w), lambda b, g: (b, nb + g)),
                  pl.BlockSpec((LRU_CONV, w), lambda b, g: (0, g)),
                  pl.BlockSpec((1, w), lambda b, g: (0, g)),
                  pl.BlockSpec((LRU_SLABS, bd, 4 * bd), lambda b, g: (g, 0, 0)),
                  pl.BlockSpec((LRU_SLABS, 1, 4 * bd), lambda b, g: (g, 0, 0)),
                  pl.BlockSpec((2, w), lambda b, g: (0, g))],
        out_specs=pl.BlockSpec((tp, w), lambda b, g: (b, g)),
        out_shape=jax.ShapeDtypeStruct((n, D_MODEL), _BF16),
        scratch_shapes=[pltpu.VMEM((tp + CONV_HALO, bd), _F32)] + [pltpu.VMEM((LRU_SLABS, t, bd), _F32)] * 4,
        compiler_params=_params(("parallel", "parallel")),
    )(gates, gates, cw, cb, wg, bg, lam)


def _mix_kernel(o_ref, y_ref, gm_ref, gl_ref, h_ref, wom_ref, wol_ref, wout_ref, g_ref, b_ref, out_ref, *, t, tp):
    tm = h_ref.shape[0]
    y_mla = _dot(o_ref[...], wom_ref[...])
    y_lru = _dot(y_ref[...], wol_ref[...])
    z = (jax.nn.sigmoid(gm_ref[...].astype(_F32)) * y_mla
         + jax.nn.sigmoid(gl_ref[...].astype(_F32)) * y_lru)
    r = DN_ALPHA * h_ref[...] + _dot(z.astype(_BF16), wout_ref[...])
    out = _layer_norm(r, g_ref[...], b_ref[...])
    out_ref[...] = jnp.where(_real_rows(pl.program_id(0), tm, t, tp), out, 0.0)


def _mix(o, y, gates, h, wom, wol, wout, g, b, t, tp):
    n, d = h.shape
    tm = TM_MIX
    tile = lambda i: (i, 0)
    return pl.pallas_call(
        functools.partial(_mix_kernel, t=t, tp=tp),
        name="mixer_out",
        grid=(n // tm,),
        in_specs=[pl.BlockSpec((tm, d), tile),
                  pl.BlockSpec((tm, d), tile),
                  pl.BlockSpec((tm, d), lambda i: (i, 2)),
                  pl.BlockSpec((tm, d), lambda i: (i, 3)),
                  pl.BlockSpec((tm, d), tile),
                  _resident((d, d)), _resident((d, d)), _resident((d, d)),
                  _resident((1, d)), _resident((1, d))],
        out_specs=pl.BlockSpec((tm, d), tile),
        out_shape=jax.ShapeDtypeStruct((n, d), _F32),
        compiler_params=_params(("parallel",)),
    )(o, y, gates, gates, h, wom, wol, wout, g, b)


def _ffn_kernel(h_ref, hp_ref, hn_ref, wu_ref, cw_ref, cb_ref, wd_ref, g_ref, b_ref, out_ref,
                lhs_ref, up0_ref, up1_ref, act_ref, *, t, tp):
    tm = h_ref.shape[0]
    halo = FFN_HALO
    d_ff = act_ref.shape[1]
    n_slab = up0_ref.shape[0] // 2
    ck = n_slab * LANES
    nj = d_ff // ck
    i = pl.program_id(0)

    lhs_ref[0:halo, :] = hp_ref[...].astype(_BF16)
    lhs_ref[halo:halo + tm, :] = h_ref[...].astype(_BF16)
    lhs_ref[halo + tm:, :] = hn_ref[...].astype(_BF16)

    @pl.when(i == 0)
    def _():
        lhs_ref[0:halo, :] = jnp.zeros((halo, lhs_ref.shape[1]), _BF16)

    def up(j, up_ref):
        lhs = lhs_ref[...]
        for half in range(2):
            c0 = half * d_ff + j * ck
            res = _dot(lhs, wu_ref[:, c0:c0 + ck])
            for s in range(n_slab):
                up_ref[half * n_slab + s] = res[:, s * LANES:(s + 1) * LANES]

    def conv(up_ref, slab, c0):
        cols = slice(c0, c0 + LANES)
        out = cb_ref[:, cols]
        for tap in range(FFN_CONV):
            off = halo + tap - FFN_CONV // 2
            out = out + up_ref[slab, off:off + tm, :] * cw_ref[tap:tap + 1, cols]
        return out

    def act(j, up_ref):
        for s in range(n_slab):
            c0 = j * ck + s * LANES
            gate = conv(up_ref, s, c0)
            val = conv(up_ref, n_slab + s, d_ff + c0)
            act_ref[:, c0:c0 + LANES] = (jax.nn.gelu(gate) * val).astype(_BF16)

    bufs = (up0_ref, up1_ref)
    up(0, bufs[0])
    for j in range(nj):
        if j + 1 < nj:
            up(j + 1, bufs[(j + 1) % 2])
        act(j, bufs[j % 2])

    f = _dot(act_ref[:, 0:ck], wd_ref[0:ck, :])
    for j in range(1, nj):
        f = f + _dot(act_ref[:, j * ck:(j + 1) * ck], wd_ref[j * ck:(j + 1) * ck, :])
    out = _layer_norm(DN_ALPHA * h_ref[...] + f, g_ref[...], b_ref[...])
    out_ref[...] = jnp.where(_real_rows(i, tm, t, tp), out, 0.0)


def _ffn(h, wu, cw, cb, wd, g, b, t, tp):
    n, d = h.shape
    tm = TM_FFN
    ck = FF_CHUNK
    d_ff = wd.shape[0]
    assert d_ff % ck == 0 and ck % LANES == 0
    halo = FFN_HALO
    hb = tm // halo
    last_halo = n // halo - 1
    up_scratch = pltpu.VMEM((2 * ck // LANES, tm + 2 * halo, LANES), _F32)
    return pl.pallas_call(
        functools.partial(_ffn_kernel, t=t, tp=tp),
        name="convffn",
        grid=(n // tm,),
        in_specs=[pl.BlockSpec((tm, d), lambda i: (i, 0)),
                  pl.BlockSpec((halo, d), lambda i: (jnp.maximum(i * hb - 1, 0), 0)),
                  pl.BlockSpec((halo, d), lambda i: (jnp.minimum((i + 1) * hb, last_halo), 0)),
                  _resident(wu.shape), _resident(cw.shape), _resident(cb.shape), _resident(wd.shape),
                  _resident((1, d)), _resident((1, d))],
        out_specs=pl.BlockSpec((tm, d), lambda i: (i, 0)),
        out_shape=jax.ShapeDtypeStruct((n, d), _F32),
        scratch_shapes=[pltpu.VMEM((tm + 2 * halo, d), _BF16), up_scratch, up_scratch,
                        pltpu.VMEM((tm, d_ff), _BF16)],
        compiler_params=_params(("parallel",)),
    )(h, h, h, wu, cw, cb, wd, g, b)


def _swap_halves(w):
    half = w.shape[-1] // 2
    return jnp.concatenate([w[..., half:], w[..., :half]], axis=-1)


def kernel(x, meta_tokens, ln0_g, ln0_b, w_in, q_norm, kv_norm, w_uq, w_uk, w_uv, w_o_mla, lru_conv_w, lru_conv_b, w_rg, b_rg, w_ig, b_ig, lru_lambda, w_o_lru, w_out, ln1_g, ln1_b, w_up, ffn_conv_w, ffn_conv_b, w_down, ln2_g, ln2_b):
    bsz, seq, d = x.shape
    t = N_META + seq
    tp = -(-t // LANES) * LANES
    n = bsz * tp
    assert d == D_MODEL and tp > t >= max(TM_LN0, TM_PROJ, TM_MIX, TM_FFN)
    assert n % TM_LN0 == 0 and n % TM_PROJ == 0 and n % TM_MIX == 0 and n % TM_FFN == 0
    row = lambda v: v.reshape(1, -1)

    meta = jnp.broadcast_to(meta_tokens[None].astype(x.dtype), (bsz, N_META, d))
    pad = jnp.zeros((bsz, tp - t, d), x.dtype)
    h = _ln0(jnp.concatenate([meta, x, pad], axis=1).reshape(n, d), row(ln0_g), row(ln0_b), t, tp)

    half = QK_ROPE // 2
    inv_freq = jnp.exp(-math.log(ROPE_THETA) * jnp.arange(half, dtype=_F32) / half)
    ang = jnp.arange(tp, dtype=_F32)[:, None] * inv_freq[None, :]
    cos, sin = jnp.cos(ang), jnp.sin(ang)
    cs = jnp.tile(jnp.concatenate([cos, cos, -sin, sin], axis=-1), (bsz, 1))
    cst = jnp.tile(jnp.concatenate([cos, sin], axis=-1).T, (1, bsz))

    depth = w_in.shape[0]
    n_small = Q_RANK + KV_RANK + QK_ROPE
    w_in_b = w_in.astype(_BF16)
    ws_all = jnp.concatenate([w_in_b[..., :n_small], _swap_halves(w_in_b[..., Q_RANK + KV_RANK:n_small])], axis=-1)
    wg_all = w_in_b[..., n_small:]
    wqt_all = jnp.swapaxes(w_uq.astype(_BF16).reshape(depth, Q_RANK, MLA_HEADS * QK_DIM), 1, 2)
    wuk_all = w_uk.astype(_BF16).reshape(depth, KV_RANK, MLA_HEADS * QK_NOPE)
    wuvt_all = jnp.swapaxes(w_uv.astype(_BF16).reshape(depth, KV_RANK, MLA_HEADS * V_HEAD), 1, 2)
    w_gate_all = jnp.concatenate([w_rg[:, 0], w_rg[:, 1], w_ig[:, 0], w_ig[:, 1]], axis=-1).astype(_BF16)
    b_gate_all = jnp.concatenate([b_rg.reshape(depth, 2, LRU_BLOCKS, LRU_BLOCK_DIM),
                                  b_ig.reshape(depth, 2, LRU_BLOCKS, LRU_BLOCK_DIM)], axis=1)
    b_gate_all = jnp.transpose(b_gate_all, (0, 2, 1, 3)).reshape(depth, LRU_BLOCKS, 1, 4 * LRU_BLOCK_DIM)
    wom_all, wol_all, wout_all = w_o_mla.astype(_BF16), w_o_lru.astype(_BF16), w_out.astype(_BF16)
    wup_all, wdown_all = w_up.astype(_BF16), w_down.astype(_BF16)

    for l in range(depth):
        qt, k, vt, gates = _proj(h, ws_all[l], wg_all[l], row(q_norm[l]), row(kv_norm[l]),
                                 wqt_all[l], wuk_all[l], wuvt_all[l], cs, cst)
        o = _attention(qt, k, vt, bsz, t, tp)
        y = _lru(gates, lru_conv_w[l], row(lru_conv_b[l]), w_gate_all[l], b_gate_all[l], lru_lambda[l], bsz, t, tp)
        h = _mix(o, y, gates, h, wom_all[l], wol_all[l], wout_all[l], row(ln1_g[l]), row(ln1_b[l]), t, tp)
        h = _ffn(h, wup_all[l], ffn_conv_w[l], row(ffn_conv_b[l]), wdown_all[l], row(ln2_g[l]), row(ln2_b[l]), t, tp)

    return h.reshape(bsz, tp, d)[:, N_META:t]
```

```python
import functools
import math

import jax
import jax.numpy as jnp
from jax import lax
from jax.experimental import pallas as pl
from jax.experimental.pallas import tpu as pltpu

D_MODEL = 1024
N_META = 16
MLA_HEADS = 8
QK_NOPE = 128
QK_ROPE = 64
QK_DIM = QK_NOPE + QK_ROPE
V_HEAD = 128
Q_RANK = 256
KV_RANK = 128
ROPE_THETA = 10000.0
LRU_BLOCKS = 8
LRU_BLOCK_DIM = 128
LRU_CONV = 4
LRU_C = 8.0
D_FF = 2816
FFN_CONV = 3
DEPTH = 2
DN_ALPHA = (2.0 * DEPTH) ** 0.25
LN_EPS = 1e-5
RMS_EPS = 1e-6

SMALL_COLS = Q_RANK + KV_RANK + 2 * QK_ROPE
GATE_COLS = 4 * D_MODEL

LANES = 128
SUBLANES = 8
BF16_ROWS = 16
VMEM_LIMIT = 56 * 1024 * 1024

V_ROWS = V_HEAD + BF16_ROWS
TM_LN0 = 1024
TM_PROJ = 512
TM_MIX = 1024
MIX_PARTS = 4
TM_FFN = 512
FF_CHUNK = 256
TQ = 512
TK = 256
LRU_SLABS = 2
SCAN_CHUNKS = 2 * SUBLANES
CONV_HALO = SUBLANES
FFN_HALO = BF16_ROWS

_BF16 = jnp.bfloat16
_F32 = jnp.float32


def _params(sem):
    return pltpu.CompilerParams(dimension_semantics=sem, vmem_limit_bytes=VMEM_LIMIT)


def _resident(shape):
    zeros = (0,) * len(shape)
    return pl.BlockSpec(shape, lambda *_: zeros, pipeline_mode=pl.Buffered(1))


def _layer_norm(x, g, b):
    mu = jnp.mean(x, axis=-1, keepdims=True)
    xc = x - mu
    var = jnp.mean(xc * xc, axis=-1, keepdims=True)
    return xc * lax.rsqrt(var + LN_EPS) * g + b


def _rms_norm(x, g):
    ms = jnp.mean(x * x, axis=-1, keepdims=True)
    return x * lax.rsqrt(ms + RMS_EPS) * g


def _one_minus_sq(log_a, a):
    y = 2.0 * log_a
    u = a * a
    near_one = jnp.where(u == 1.0, y, (u - 1.0) * y * pl.reciprocal(jnp.log(u), approx=True))
    return -jnp.where(u > 0.5, near_one, u - 1.0)


def _sqrt(x):
    return jnp.where(x > 0.0, x * lax.rsqrt(x), 0.0)


def _dot(a, b):
    return jnp.dot(a, b, preferred_element_type=_F32)


def _dot_nt(a, b):
    return lax.dot_general(a, b, (((1,), (1,)), ((), ())), preferred_element_type=_F32)


def _real_rows(i, tm, t, tp):
    pos = lax.rem(i * tm, tp) + lax.broadcasted_iota(jnp.int32, (tm, 1), 0)
    return (pos < t) | (pos >= tp)


def _ln0_kernel(x_ref, g_ref, b_ref, o_ref, *, t, tp):
    tm = x_ref.shape[0]
    y = _layer_norm(x_ref[...], g_ref[...], b_ref[...])
    o_ref[...] = jnp.where(_real_rows(pl.program_id(0), tm, t, tp), y, 0.0)


def _ln0(x, g, b, t, tp):
    n, d = x.shape
    tm = TM_LN0
    return pl.pallas_call(
        functools.partial(_ln0_kernel, t=t, tp=tp),
        name="ln0",
        grid=(n // tm,),
        in_specs=[pl.BlockSpec((tm, d), lambda i: (i, 0)), _resident((1, d)), _resident((1, d))],
        out_specs=pl.BlockSpec((tm, d), lambda i: (i, 0)),
        out_shape=jax.ShapeDtypeStruct((n, d), _F32),
        compiler_params=_params(("parallel",)),
    )(x, g, b)


def _proj_kernel(h_ref, ws_ref, wg_ref, qn_ref, kvn_ref, wqt_ref, wuk_ref, wuvt_ref, cs_ref, cst_ref,
                 qt_ref, k_ref, vt_ref, g_ref):
    tm = h_ref.shape[0]
    hb = h_ref[...].astype(_BF16)
    small = _dot(hb, ws_ref[...])
    cq = _rms_norm(small[:, :Q_RANK], qn_ref[...]).astype(_BF16)
    ckv = _rms_norm(small[:, Q_RANK:Q_RANK + KV_RANK], kvn_ref[...]).astype(_BF16)

    prod = small[:, Q_RANK + KV_RANK:] * cs_ref[...]
    k_rope = (prod + pltpu.roll(prod, QK_ROPE, axis=1))[:, :QK_ROPE].astype(_BF16)
    k_nope = _dot(ckv, wuk_ref[...])
    for hd in range(MLA_HEADS):
        k_ref[hd, :, :QK_NOPE] = k_nope[:, hd * QK_NOPE:(hd + 1) * QK_NOPE].astype(_BF16)
        k_ref[hd, :, QK_NOPE:] = k_rope

    half = QK_ROPE // 2
    qt = _dot_nt(wqt_ref[...], cq) * (1.0 / math.sqrt(QK_DIM))
    cos_t = cst_ref[0:half, :]
    sin_t = cst_ref[half:QK_ROPE, :]
    for hd in range(MLA_HEADS):
        r0 = hd * QK_DIM
        x1 = qt[r0 + QK_NOPE:r0 + QK_NOPE + half]
        x2 = qt[r0 + QK_NOPE + half:r0 + QK_DIM]
        qt_ref[hd, 0:QK_NOPE, :] = qt[r0:r0 + QK_NOPE].astype(_BF16)
        qt_ref[hd, QK_NOPE:QK_NOPE + half, :] = (x1 * cos_t - x2 * sin_t).astype(_BF16)
        qt_ref[hd, QK_NOPE + half:QK_DIM, :] = (x2 * cos_t + x1 * sin_t).astype(_BF16)

    vt = _dot_nt(wuvt_ref[...], ckv)
    row = lax.broadcasted_iota(jnp.int32, (V_ROWS - V_HEAD, tm), 0)
    ones_rows = jnp.where(row == 0, 1.0, 0.0).astype(_BF16)
    for hd in range(MLA_HEADS):
        vt_ref[hd, 0:V_HEAD, :] = vt[hd * V_HEAD:(hd + 1) * V_HEAD].astype(_BF16)
        vt_ref[hd, V_HEAD:V_ROWS, :] = ones_rows

    for c in range(GATE_COLS // D_MODEL):
        cols = slice(c * D_MODEL, (c + 1) * D_MODEL)
        g_ref[:, cols] = _dot(hb, wg_ref[:, cols]).astype(_BF16)


def _proj(h, ws, wg, qn, kvn, wqt, wuk, wuvt, cs, cst):
    n, d = h.shape
    tm = TM_PROJ
    heads = lambda i: (0, i, 0)
    heads_t = lambda i: (0, 0, i)
    return pl.pallas_call(
        _proj_kernel,
        name="in_proj",
        grid=(n // tm,),
        in_specs=[pl.BlockSpec((tm, d), lambda i: (i, 0)),
                  _resident(ws.shape), _resident(wg.shape), _resident(qn.shape), _resident(kvn.shape),
                  _resident(wqt.shape), _resident(wuk.shape), _resident(wuvt.shape),
                  pl.BlockSpec((tm, 2 * QK_ROPE), lambda i: (i, 0)),
                  pl.BlockSpec((QK_ROPE, tm), lambda i: (0, i))],
        out_specs=[pl.BlockSpec((MLA_HEADS, QK_DIM, tm), heads_t),
                   pl.BlockSpec((MLA_HEADS, tm, QK_DIM), heads),
                   pl.BlockSpec((MLA_HEADS, V_ROWS, tm), heads_t),
                   pl.BlockSpec((tm, GATE_COLS), lambda i: (i, 0))],
        out_shape=[jax.ShapeDtypeStruct((MLA_HEADS, QK_DIM, n), _BF16),
                   jax.ShapeDtypeStruct((MLA_HEADS, n, QK_DIM), _BF16),
                   jax.ShapeDtypeStruct((MLA_HEADS, V_ROWS, n), _BF16),
                   jax.ShapeDtypeStruct((n, GATE_COLS), _BF16)],
        compiler_params=_params(("parallel",)),
    )(h, ws, wg, qn, kvn, wqt, wuk, wuvt, cs, cst)


def _scores(k_ref, q_t, s_ref, t, tp):
    tq = q_t.shape[1]
    s_ref[:, 0:tq] = _dot(k_ref[...], q_t)
    s_ref[t:tp, 0:tq] = jnp.full((tp - t, tq), -jnp.inf, _F32)
    return jnp.max(s_ref[:, 0:tq], axis=0, keepdims=True)


def _values(vt_ref, s_ref, m, tq, tp):
    chunks = [(r0, min(TK, tp - r0)) for r0 in range(0, tp, TK)]
    acc = jnp.zeros((V_ROWS, tq), _F32)
    for r0, rows in chunks:
        p = jnp.exp(s_ref[r0:r0 + rows, 0:tq] - m).astype(_BF16)
        acc = acc + _dot(vt_ref[:, r0:r0 + rows], p)
    return acc[:V_HEAD] / acc[V_HEAD:V_HEAD + 1]


def _attn_kernel(qt_ref, k_ref, vt_ref, o_ref, s0_ref, s1_ref, *, t, tp):
    n_tiles = tp // TQ
    tail_q = tp - n_tiles * TQ

    def scores(start, tq, s_ref):
        return _scores(k_ref, qt_ref[:, pl.ds(start, tq)], s_ref, t, tp)

    def values(start, tq, s_ref, m):
        o_ref[pl.ds(start, tq), :] = _values(vt_ref, s_ref, m, tq, tp).T.astype(o_ref.dtype)

    def tile(i):
        return pl.multiple_of(i * TQ, TQ)

    m0 = scores(0, TQ, s0_ref)

    def body(j, m0):
        m1 = scores(tile(2 * j + 1), TQ, s1_ref)
        values(tile(2 * j), TQ, s0_ref, m0)
        m0 = scores(tile(2 * j + 2), TQ, s0_ref)
        values(tile(2 * j + 1), TQ, s1_ref, m1)
        return m0

    m0 = lax.fori_loop(0, n_tiles // 2 - 1, body, m0)
    m1 = scores((n_tiles - 1) * TQ, TQ, s1_ref)
    values((n_tiles - 2) * TQ, TQ, s0_ref, m0)
    m0 = scores(n_tiles * TQ, tail_q, s0_ref)
    values((n_tiles - 1) * TQ, TQ, s1_ref, m1)
    values(n_tiles * TQ, tail_q, s0_ref, m0)


def _attention(qt, k, vt, bsz, t, tp):
    n = k.shape[1]
    assert (tp // TQ) % 2 == 0 and 0 < tp % TQ <= TQ and tp % TQ % LANES == 0 and tp % TK % BF16_ROWS == 0
    return pl.pallas_call(
        functools.partial(_attn_kernel, t=t, tp=tp),
        name="mla_attention",
        grid=(bsz, MLA_HEADS),
        in_specs=[pl.BlockSpec((None, QK_DIM, tp), lambda b, hd: (hd, 0, b)),
                  pl.BlockSpec((None, tp, QK_DIM), lambda b, hd: (hd, b, 0)),
                  pl.BlockSpec((None, V_ROWS, tp), lambda b, hd: (hd, 0, b))],
        out_specs=pl.BlockSpec((tp, V_HEAD), lambda b, hd: (b, hd)),
        out_shape=jax.ShapeDtypeStruct((n, MLA_HEADS * V_HEAD), _BF16),
        scratch_shapes=[pltpu.VMEM((tp, TQ), _F32)] * 2,
        compiler_params=_params(("parallel", "parallel")),
    )(qt, k, vt)


def _lru_kernel(g_ref, x_ref, cw_ref, cb_ref, wg_ref, bg_ref, lam_ref, y_ref,
                xp_ref, af_ref, uf_ref, ab_ref, ub_ref, *, t, tp):
    bd = LRU_BLOCK_DIM
    n_chunk = SCAN_CHUNKS
    clen = t // n_chunk
    lam = lam_ref[...]
    sp = jnp.maximum(-lam, 0.0) + jnp.log1p(jnp.exp(-jnp.abs(lam)))

    for s in range(LRU_SLABS):
        lanes = slice(s * bd, (s + 1) * bd)
        xp_ref[0:CONV_HALO, :] = jnp.zeros((CONV_HALO, bd), _F32)
        xp_ref[CONV_HALO:CONV_HALO + tp, :] = x_ref[:, lanes].astype(_F32)
        xc = cb_ref[:, lanes]
        for tap in range(LRU_CONV):
            off = CONV_HALO + tap - LRU_CONV // 2
            xc = xc + xp_ref[off:off + t, :] * cw_ref[tap:tap + 1, lanes]

        th = jnp.tanh(_dot(xc.astype(_BF16), wg_ref[s]) + bg_ref[s])
        half_xc = 0.5 * xc
        for d, (a_ref, u_ref) in enumerate(((af_ref, uf_ref), (ab_ref, ub_ref))):
            th_r = th[:, d * bd:(d + 1) * bd]
            th_i = th[:, (2 + d) * bd:(3 + d) * bd]
            c = (-0.5 * LRU_C) * sp[d:d + 1, lanes]
            log_a = c * th_r + c
            a = jnp.exp(log_a)
            a_ref[s] = a
            u_ref[s] = _sqrt(_one_minus_sq(log_a, a)) * (half_xc * th_i + half_xc)

    def rows(j):
        return pl.ds(j, n_chunk, stride=clen)

    def pass1(j, carry):
        jb = clen - 1 - j
        out = []
        for s in range(LRU_SLABS):
            hf, pf, hb, pb = carry[4 * s:4 * s + 4]
            a = af_ref[s, rows(j), :]
            hf = a * hf + uf_ref[s, rows(j), :]
            pf = pf * a
            uf_ref[s, rows(j), :] = hf
            af_ref[s, rows(j), :] = pf
            a = ab_ref[s, rows(jb), :]
            hb = a * hb + ub_ref[s, rows(jb), :]
            pb = pb * a
            ub_ref[s, rows(jb), :] = hb
            ab_ref[s, rows(jb), :] = pb
            out += [hf, pf, hb, pb]
        return tuple(out)

    zero = jnp.zeros((n_chunk, bd), _F32)
    one = jnp.ones((n_chunk, bd), _F32)
    ends = lax.fori_loop(0, clen, pass1, (zero, one, zero, one) * LRU_SLABS)

    carries = []
    for s in range(LRU_SLABS):
        hf, pf, hb, pb = ends[4 * s:4 * s + 4]
        cf = [jnp.zeros((1, bd), _F32)]
        for c in range(1, n_chunk):
            cf.append(hf[c - 1:c, :] + pf[c - 1:c, :] * cf[c - 1])
        cb = [jnp.zeros((1, bd), _F32)]
        for c in range(n_chunk - 2, -1, -1):
            cb.insert(0, hb[c + 1:c + 2, :] + pb[c + 1:c + 2, :] * cb[0])
        carries.append((jnp.concatenate(cf, axis=0), jnp.concatenate(cb, axis=0)))

    def pass2(j, carry):
        for s in range(LRU_SLABS):
            cf, cb = carries[s]
            uf_ref[s, rows(j), :] = (uf_ref[s, rows(j), :] + af_ref[s, rows(j), :] * cf
                                     + ub_ref[s, rows(j), :] + ab_ref[s, rows(j), :] * cb)
        return carry

    lax.fori_loop(0, clen, pass2, 0, unroll=2)
    for s in range(LRU_SLABS):
        lanes = slice(s * bd, (s + 1) * bd)
        y_ref[0:t, lanes] = (jax.nn.gelu(g_ref[0:t, lanes].astype(_F32)) * uf_ref[s]).astype(y_ref.dtype)
    y_ref[t:tp, :] = jnp.zeros((tp - t, LRU_SLABS * bd), y_ref.dtype)


def _lru(gates, cw, cb, wg, bg, lam, bsz, t, tp):
    n = gates.shape[0]
    bd = LRU_BLOCK_DIM
    w = LRU_SLABS * bd
    nb = D_MODEL // w
    assert t % SCAN_CHUNKS == 0
    return pl.pallas_call(
        functools.partial(_lru_kernel, t=t, tp=tp),
        name="rg_lru",
        grid=(bsz, nb),
        in_specs=[pl.BlockSpec((tp, w), lambda b, g: (b, g)),
                  pl.BlockSpec((tp, w), lambda b, g: (b, nb + g)),
                  pl.BlockSpec((LRU_CONV, w), lambda b, g: (0, g)),
                  pl.BlockSpec((1, w), lambda b, g: (0, g)),
                  pl.BlockSpec((LRU_SLABS, bd, 4 * bd), lambda b, g: (g, 0, 0)),
                  pl.BlockSpec((LRU_SLABS, 1, 4 * bd), lambda b, g: (g, 0, 0)),
                  pl.BlockSpec((2, w), lambda b, g: (0, g))],
        out_specs=pl.BlockSpec((tp, w), lambda b, g: (b, g)),
        out_shape=jax.ShapeDtypeStruct((n, D_MODEL), _BF16),
        scratch_shapes=[pltpu.VMEM((tp + CONV_HALO, bd), _F32)] + [pltpu.VMEM((LRU_SLABS, t, bd), _F32)] * 4,
        compiler_params=_params(("parallel", "parallel")),
    )(gates, gates, cw, cb, wg, bg, lam)


def _mix_kernel(o_ref, y_ref, gm_ref, gl_ref, h_ref, wom_ref, wol_ref, wout_ref, g_ref, b_ref, out_ref, *, t, tp):
    tm = h_ref.shape[0]
    real = _real_rows(pl.program_id(0), tm, t, tp)
    rp = tm // MIX_PARTS
    for p in range(MIX_PARTS):
        rows = slice(p * rp, (p + 1) * rp)
        y_mla = _dot(o_ref[rows, :], wom_ref[...])
        y_lru = _dot(y_ref[rows, :], wol_ref[...])
        z = (jax.nn.sigmoid(gm_ref[rows, :].astype(_F32)) * y_mla
             + jax.nn.sigmoid(gl_ref[rows, :].astype(_F32)) * y_lru)
        r = DN_ALPHA * h_ref[rows, :] + _dot(z.astype(_BF16), wout_ref[...])
        out = _layer_norm(r, g_ref[...], b_ref[...])
        out_ref[rows, :] = jnp.where(real[rows], out, 0.0)


def _mix(o, y, gates, h, wom, wol, wout, g, b, t, tp):
    n, d = h.shape
    tm = TM_MIX
    tile = lambda i: (i, 0)
    return pl.pallas_call(
        functools.partial(_mix_kernel, t=t, tp=tp),
        name="mixer_out",
        grid=(n // tm,),
        in_specs=[pl.BlockSpec((tm, d), tile),
                  pl.BlockSpec((tm, d), tile),
                  pl.BlockSpec((tm, d), lambda i: (i, 2)),
                  pl.BlockSpec((tm, d), lambda i: (i, 3)),
                  pl.BlockSpec((tm, d), tile),
                  _resident((d, d)), _resident((d, d)), _resident((d, d)),
                  _resident((1, d)), _resident((1, d))],
        out_specs=pl.BlockSpec((tm, d), tile),
        out_shape=jax.ShapeDtypeStruct((n, d), _F32),
        compiler_params=_params(("parallel",)),
    )(o, y, gates, gates, h, wom, wol, wout, g, b)


def _ffn_kernel(h_ref, hp_ref, hn_ref, wu_ref, cw_ref, cb_ref, wd_ref, g_ref, b_ref, out_ref,
                lhs_ref, up0_ref, up1_ref, act_ref, *, t, tp):
    tm = h_ref.shape[0]
    halo = FFN_HALO
    d_ff = act_ref.shape[1]
    n_slab = up0_ref.shape[0] // 2
    ck = n_slab * LANES
    nj = d_ff // ck
    i = pl.program_id(0)

    lhs_ref[0:halo, :] = hp_ref[...].astype(_BF16)
    lhs_ref[halo:halo + tm, :] = h_ref[...].astype(_BF16)
    lhs_ref[halo + tm:, :] = hn_ref[...].astype(_BF16)

    @pl.when(i == 0)
    def _():
        lhs_ref[0:halo, :] = jnp.zeros((halo, lhs_ref.shape[1]), _BF16)

    def up(j, up_ref):
        lhs = lhs_ref[...]
        for half in range(2):
            c0 = half * d_ff + j * ck
            res = _dot(lhs, wu_ref[:, c0:c0 + ck])
            for s in range(n_slab):
                up_ref[half * n_slab + s] = res[:, s * LANES:(s + 1) * LANES]

    def conv(up_ref, slab, c0):
        cols = slice(c0, c0 + LANES)
        out = cb_ref[:, cols]
        for tap in range(FFN_CONV):
            off = halo + tap - FFN_CONV // 2
            out = out + up_ref[slab, off:off + tm, :] * cw_ref[tap:tap + 1, cols]
        return out

    def act(j, up_ref):
        for s in range(n_slab):
            c0 = j * ck + s * LANES
            gate = conv(up_ref, s, c0)
            val = conv(up_ref, n_slab + s, d_ff + c0)
            act_ref[:, c0:c0 + LANES] = (jax.nn.gelu(gate) * val).astype(_BF16)

    bufs = (up0_ref, up1_ref)
    up(0, bufs[0])
    for j in range(nj):
        if j + 1 < nj:
            up(j + 1, bufs[(j + 1) % 2])
        act(j, bufs[j % 2])

    f = _dot(act_ref[:, 0:ck], wd_ref[0:ck, :])
    for j in range(1, nj):
        f = f + _dot(act_ref[:, j * ck:(j + 1) * ck], wd_ref[j * ck:(j + 1) * ck, :])
    out = _layer_norm(DN_ALPHA * h_ref[...] + f, g_ref[...], b_ref[...])
    out_ref[...] = jnp.where(_real_rows(i, tm, t, tp), out, 0.0)


def _ffn(h, wu, cw, cb, wd, g, b, t, tp):
    n, d = h.shape
    tm = TM_FFN
    ck = FF_CHUNK
    d_ff = wd.shape[0]
    assert d_ff % ck == 0 and ck % LANES == 0
    halo = FFN_HALO
    hb = tm // halo
    last_halo = n // halo - 1
    up_scratch = pltpu.VMEM((2 * ck // LANES, tm + 2 * halo, LANES), _F32)
    return pl.pallas_call(
        functools.partial(_ffn_kernel, t=t, tp=tp),
        name="convffn",
        grid=(n // tm,),
        in_specs=[pl.BlockSpec((tm, d), lambda i: (i, 0)),
                  pl.BlockSpec((halo, d), lambda i: (jnp.maximum(i * hb - 1, 0), 0)),
                  pl.BlockSpec((halo, d), lambda i: (jnp.minimum((i + 1) * hb, last_halo), 0)),
                  _resident(wu.shape), _resident(cw.shape), _resident(cb.shape), _resident(wd.shape),
                  _resident((1, d)), _resident((1, d))],
        out_specs=pl.BlockSpec((tm, d), lambda i: (i, 0)),
        out_shape=jax.ShapeDtypeStruct((n, d), _F32),
        scratch_shapes=[pltpu.VMEM((tm + 2 * halo, d), _BF16), up_scratch, up_scratch,
                        pltpu.VMEM((tm, d_ff), _BF16)],
        compiler_params=_params(("parallel",)),
    )(h, h, h, wu, cw, cb, wd, g, b)


def _swap_halves(w):
    half = w.shape[-1] // 2
    return jnp.concatenate([w[..., half:], w[..., :half]], axis=-1)


def kernel(x, meta_tokens, ln0_g, ln0_b, w_in, q_norm, kv_norm, w_uq, w_uk, w_uv, w_o_mla, lru_conv_w, lru_conv_b, w_rg, b_rg, w_ig, b_ig, lru_lambda, w_o_lru, w_out, ln1_g, ln1_b, w_up, ffn_conv_w, ffn_conv_b, w_down, ln2_g, ln2_b):
    bsz, seq, d = x.shape
    t = N_META + seq
    tp = -(-t // LANES) * LANES
    n = bsz * tp
    assert d == D_MODEL and tp > t >= max(TM_LN0, TM_PROJ, TM_MIX, TM_FFN)
    assert n % TM_LN0 == 0 and n % TM_PROJ == 0 and n % TM_MIX == 0 and n % TM_FFN == 0
    row = lambda v: v.reshape(1, -1)

    meta = jnp.broadcast_to(meta_tokens[None].astype(x.dtype), (bsz, N_META, d))
    pad = jnp.zeros((bsz, tp - t, d), x.dtype)
    h = _ln0(jnp.concatenate([meta, x, pad], axis=1).reshape(n, d), row(ln0_g), row(ln0_b), t, tp)

    half = QK_ROPE // 2
    inv_freq = jnp.exp(-math.log(ROPE_THETA) * jnp.arange(half, dtype=_F32) / half)
    ang = jnp.arange(tp, dtype=_F32)[:, None] * inv_freq[None, :]
    cos, sin = jnp.cos(ang), jnp.sin(ang)
    cs = jnp.tile(jnp.concatenate([cos, cos, -sin, sin], axis=-1), (bsz, 1))
    cst = jnp.tile(jnp.concatenate([cos, sin], axis=-1).T, (1, bsz))

    depth = w_in.shape[0]
    n_small = Q_RANK + KV_RANK + QK_ROPE
    w_in_b = w_in.astype(_BF16)
    ws_all = jnp.concatenate([w_in_b[..., :n_small], _swap_halves(w_in_b[..., Q_RANK + KV_RANK:n_small])], axis=-1)
    wg_all = w_in_b[..., n_small:]
    wqt_all = jnp.swapaxes(w_uq.astype(_BF16).reshape(depth, Q_RANK, MLA_HEADS * QK_DIM), 1, 2)
    wuk_all = w_uk.astype(_BF16).reshape(depth, KV_RANK, MLA_HEADS * QK_NOPE)
    wuvt_all = jnp.swapaxes(w_uv.astype(_BF16).reshape(depth, KV_RANK, MLA_HEADS * V_HEAD), 1, 2)
    w_gate_all = (0.5 * jnp.concatenate([w_rg[:, 0], w_rg[:, 1], w_ig[:, 0], w_ig[:, 1]], axis=-1)).astype(_BF16)
    b_gate_all = jnp.concatenate([b_rg.reshape(depth, 2, LRU_BLOCKS, LRU_BLOCK_DIM),
                                  b_ig.reshape(depth, 2, LRU_BLOCKS, LRU_BLOCK_DIM)], axis=1)
    b_gate_all = 0.5 * jnp.transpose(b_gate_all, (0, 2, 1, 3)).reshape(depth, LRU_BLOCKS, 1, 4 * LRU_BLOCK_DIM)
    wom_all, wol_all, wout_all = w_o_mla.astype(_BF16), w_o_lru.astype(_BF16), w_out.astype(_BF16)
    wup_all, wdown_all = w_up.astype(_BF16), w_down.astype(_BF16)

    for l in range(depth):
        qt, k, vt, gates = _proj(h, ws_all[l], wg_all[l], row(q_norm[l]), row(kv_norm[l]),
                                 wqt_all[l], wuk_all[l], wuvt_all[l], cs, cst)
        o = _attention(qt, k, vt, bsz, t, tp)
        y = _lru(gates, lru_conv_w[l], row(lru_conv_b[l]), w_gate_all[l], b_gate_all[l], lru_lambda[l], bsz, t, tp)
        h = _mix(o, y, gates, h, wom_all[l], wol_all[l], wout_all[l], row(ln1_g[l]), row(ln1_b[l]), t, tp)
        h = _ffn(h, wup_all[l], ffn_conv_w[l], row(ffn_conv_b[l]), wdown_all[l], row(ln2_g[l]), row(ln2_b[l]), t, tp)

    return h.reshape(bsz, tp, d)[:, N_META:t]
```

```python
import functools
import math

import jax
import jax.numpy as jnp
from jax import lax
from jax.experimental import pallas as pl
from jax.experimental.pallas import tpu as pltpu

D_MODEL = 1024
N_META = 16
MLA_HEADS = 8
QK_NOPE = 128
QK_ROPE = 64
QK_DIM = QK_NOPE + QK_ROPE
V_HEAD = 128
Q_RANK = 256
KV_RANK = 128
ROPE_THETA = 10000.0
LRU_BLOCKS = 8
LRU_BLOCK_DIM = 128
LRU_CONV = 4
LRU_C = 8.0
D_FF = 2816
FFN_CONV = 3
DEPTH = 2
DN_ALPHA = (2.0 * DEPTH) ** 0.25
LN_EPS = 1e-5
RMS_EPS = 1e-6

SMALL_COLS = Q_RANK + KV_RANK + 2 * QK_ROPE
GATE_COLS = 4 * D_MODEL

LANES = 128
SUBLANES = 8
BF16_ROWS = 16
VMEM_LIMIT = 56 * 1024 * 1024

V_ROWS = V_HEAD + BF16_ROWS
TM_LN0 = 512
TM_PROJ = 512
TM_MIX = 1024
MIX_PARTS = 4
TM_FFN = 512
FF_CHUNK = 256
TQ = 512
TK = 256
LRU_SLABS = 2
SCAN_CHUNKS = 2 * SUBLANES
CONV_HALO = SUBLANES
FFN_HALO = BF16_ROWS

_BF16 = jnp.bfloat16
_F32 = jnp.float32


def _params(sem):
    return pltpu.CompilerParams(dimension_semantics=sem, vmem_limit_bytes=VMEM_LIMIT)


def _resident(shape):
    zeros = (0,) * len(shape)
    return pl.BlockSpec(shape, lambda *_: zeros, pipeline_mode=pl.Buffered(1))


def _layer_norm(x, g, b):
    mu = jnp.mean(x, axis=-1, keepdims=True)
    xc = x - mu
    var = jnp.mean(xc * xc, axis=-1, keepdims=True)
    return xc * lax.rsqrt(var + LN_EPS) * g + b


def _rms_norm(x, g):
    ms = jnp.mean(x * x, axis=-1, keepdims=True)
    return x * lax.rsqrt(ms + RMS_EPS) * g


def _one_minus_sq(log_a, a):
    y = 2.0 * log_a
    u = a * a
    near_one = jnp.where(u == 1.0, y, (u - 1.0) * y * pl.reciprocal(jnp.log(u), approx=True))
    return -jnp.where(u > 0.5, near_one, u - 1.0)


def _sqrt(x):
    return jnp.where(x > 0.0, x * lax.rsqrt(x), 0.0)


def _dot(a, b):
    return jnp.dot(a, b, preferred_element_type=_F32)


def _dot_nt(a, b):
    return lax.dot_general(a, b, (((1,), (1,)), ((), ())), preferred_element_type=_F32)


def _real_rows(i, tm, t, tp):
    pos = lax.rem(i * tm, tp) + lax.broadcasted_iota(jnp.int32, (tm, 1), 0)
    return (pos < t) | (pos >= tp)


def _ln0_kernel(meta_ref, xprev_ref, x_ref, g_ref, b_ref, o_ref, *, t):
    k = pl.program_id(1)
    tm = o_ref.shape[0]
    g, b = g_ref[...], b_ref[...]
    head = jnp.where(k == 0, meta_ref[...], xprev_ref[...])
    o_ref[0:N_META, :] = _layer_norm(head, g, b)
    body = _layer_norm(x_ref[0:tm - N_META, :], g, b)
    pos = k * tm + N_META + lax.broadcasted_iota(jnp.int32, (tm - N_META, 1), 0)
    o_ref[N_META:tm, :] = jnp.where(pos < t, body, 0.0)


def _ln0(x, meta, g, b, tp):
    bsz, seq, d = x.shape
    tm = TM_LN0
    t = N_META + seq
    assert seq % tm == 0 and tm % N_META == 0 and N_META % SUBLANES == 0 and tp - seq <= tm
    per_tile = tm // N_META
    last_x = seq // tm - 1
    return pl.pallas_call(
        functools.partial(_ln0_kernel, t=t),
        name="ln0",
        grid=(bsz, pl.cdiv(tp, tm)),
        in_specs=[_resident((N_META, d)),
                  pl.BlockSpec((None, N_META, d), lambda i, k: (i, jnp.maximum(k * per_tile - 1, 0), 0)),
                  pl.BlockSpec((None, tm, d), lambda i, k: (i, jnp.minimum(k, last_x), 0)),
                  _resident((1, d)), _resident((1, d))],
        out_specs=pl.BlockSpec((None, tm, d), lambda i, k: (i, k, 0)),
        out_shape=jax.ShapeDtypeStruct((bsz, tp, d), _F32),
        compiler_params=_params(("parallel", "parallel")),
    )(meta, x, x, g, b)


def _proj_kernel(h_ref, ws_ref, wg_ref, qn_ref, kvn_ref, wqt_ref, wuk_ref, wuvt_ref, cs_ref, cst_ref,
                 qt_ref, k_ref, vt_ref, g_ref):
    tm = h_ref.shape[0]
    hb = h_ref[...].astype(_BF16)
    small = _dot(hb, ws_ref[...])
    cq = _rms_norm(small[:, :Q_RANK], qn_ref[...]).astype(_BF16)
    ckv = _rms_norm(small[:, Q_RANK:Q_RANK + KV_RANK], kvn_ref[...]).astype(_BF16)

    prod = small[:, Q_RANK + KV_RANK:] * cs_ref[...]
    k_rope = (prod + pltpu.roll(prod, QK_ROPE, axis=1))[:, :QK_ROPE].astype(_BF16)
    k_nope = _dot(ckv, wuk_ref[...])
    for hd in range(MLA_HEADS):
        k_ref[hd, :, :QK_NOPE] = k_nope[:, hd * QK_NOPE:(hd + 1) * QK_NOPE].astype(_BF16)
        k_ref[hd, :, QK_NOPE:] = k_rope

    half = QK_ROPE // 2
    qt = _dot_nt(wqt_ref[...], cq) * (1.0 / math.sqrt(QK_DIM))
    cos_t = cst_ref[0:half, :]
    sin_t = cst_ref[half:QK_ROPE, :]
    for hd in range(MLA_HEADS):
        r0 = hd * QK_DIM
        x1 = qt[r0 + QK_NOPE:r0 + QK_NOPE + half]
        x2 = qt[r0 + QK_NOPE + half:r0 + QK_DIM]
        qt_ref[hd, 0:QK_NOPE, :] = qt[r0:r0 + QK_NOPE].astype(_BF16)
        qt_ref[hd, QK_NOPE:QK_NOPE + half, :] = (x1 * cos_t - x2 * sin_t).astype(_BF16)
        qt_ref[hd, QK_NOPE + half:QK_DIM, :] = (x2 * cos_t + x1 * sin_t).astype(_BF16)

    vt = _dot_nt(wuvt_ref[...], ckv)
    row = lax.broadcasted_iota(jnp.int32, (V_ROWS - V_HEAD, tm), 0)
    ones_rows = jnp.where(row == 0, 1.0, 0.0).astype(_BF16)
    for hd in range(MLA_HEADS):
        vt_ref[hd, 0:V_HEAD, :] = vt[hd * V_HEAD:(hd + 1) * V_HEAD].astype(_BF16)
        vt_ref[hd, V_HEAD:V_ROWS, :] = ones_rows

    for c in range(GATE_COLS // D_MODEL):
        cols = slice(c * D_MODEL, (c + 1) * D_MODEL)
        g_ref[:, cols] = _dot(hb, wg_ref[:, cols]).astype(_BF16)


def _proj(h, ws, wg, qn, kvn, wqt, wuk, wuvt, cs, cst):
    n, d = h.shape
    tm = TM_PROJ
    heads = lambda i: (0, i, 0)
    heads_t = lambda i: (0, 0, i)
    return pl.pallas_call(
        _proj_kernel,
        name="in_proj",
        grid=(n // tm,),
        in_specs=[pl.BlockSpec((tm, d), lambda i: (i, 0)),
                  _resident(ws.shape), _resident(wg.shape), _resident(qn.shape), _resident(kvn.shape),
                  _resident(wqt.shape), _resident(wuk.shape), _resident(wuvt.shape),
                  pl.BlockSpec((tm, 2 * QK_ROPE), lambda i: (i, 0)),
                  pl.BlockSpec((QK_ROPE, tm), lambda i: (0, i))],
        out_specs=[pl.BlockSpec((MLA_HEADS, QK_DIM, tm), heads_t),
                   pl.BlockSpec((MLA_HEADS, tm, QK_DIM), heads),
                   pl.BlockSpec((MLA_HEADS, V_ROWS, tm), heads_t),
                   pl.BlockSpec((tm, GATE_COLS), lambda i: (i, 0))],
        out_shape=[jax.ShapeDtypeStruct((MLA_HEADS, QK_DIM, n), _BF16),
                   jax.ShapeDtypeStruct((MLA_HEADS, n, QK_DIM), _BF16),
                   jax.ShapeDtypeStruct((MLA_HEADS, V_ROWS, n), _BF16),
                   jax.ShapeDtypeStruct((n, GATE_COLS), _BF16)],
        compiler_params=_params(("parallel",)),
    )(h, ws, wg, qn, kvn, wqt, wuk, wuvt, cs, cst)


def _scores(k_ref, q_t, s_ref, t, tp):
    tq = q_t.shape[1]
    s_ref[:, 0:tq] = _dot(k_ref[...], q_t)
    s_ref[t:tp, 0:tq] = jnp.full((tp - t, tq), -jnp.inf, _F32)
    return jnp.max(s_ref[:, 0:tq], axis=0, keepdims=True)


def _values(vt_ref, s_ref, m, tq, tp):
    chunks = [(r0, min(TK, tp - r0)) for r0 in range(0, tp, TK)]
    acc = jnp.zeros((V_ROWS, tq), _F32)
    for r0, rows in chunks:
        p = jnp.exp(s_ref[r0:r0 + rows, 0:tq] - m).astype(_BF16)
        acc = acc + _dot(vt_ref[:, r0:r0 + rows], p)
    return acc[:V_HEAD] / acc[V_HEAD:V_HEAD + 1]


def _attn_kernel(qt_ref, k_ref, vt_ref, o_ref, s0_ref, s1_ref, *, t, tp):
    n_tiles = tp // TQ
    tail_q = tp - n_tiles * TQ

    def scores(start, tq, s_ref):
        return _scores(k_ref, qt_ref[:, pl.ds(start, tq)], s_ref, t, tp)

    def values(start, tq, s_ref, m):
        o_ref[pl.ds(start, tq), :] = _values(vt_ref, s_ref, m, tq, tp).T.astype(o_ref.dtype)

    def tile(i):
        return pl.multiple_of(i * TQ, TQ)

    m0 = scores(0, TQ, s0_ref)

    def body(j, m0):
        m1 = scores(tile(2 * j + 1), TQ, s1_ref)
        values(tile(2 * j), TQ, s0_ref, m0)
        m0 = scores(tile(2 * j + 2), TQ, s0_ref)
        values(tile(2 * j + 1), TQ, s1_ref, m1)
        return m0

    m0 = lax.fori_loop(0, n_tiles // 2 - 1, body, m0)
    m1 = scores((n_tiles - 1) * TQ, TQ, s1_ref)
    values((n_tiles - 2) * TQ, TQ, s0_ref, m0)
    m0 = scores(n_tiles * TQ, tail_q, s0_ref)
    values((n_tiles - 1) * TQ, TQ, s1_ref, m1)
    values(n_tiles * TQ, tail_q, s0_ref, m0)


def _attention(qt, k, vt, bsz, t, tp):
    n = k.shape[1]
    assert (tp // TQ) % 2 == 0 and 0 < tp % TQ <= TQ and tp % TQ % LANES == 0 and tp % TK % BF16_ROWS == 0
    return pl.pallas_call(
        functools.partial(_attn_kernel, t=t, tp=tp),
        name="mla_attention",
        grid=(bsz, MLA_HEADS),
        in_specs=[pl.BlockSpec((None, QK_DIM, tp), lambda b, hd: (hd, 0, b)),
                  pl.BlockSpec((None, tp, QK_DIM), lambda b, hd: (hd, b, 0)),
                  pl.BlockSpec((None, V_ROWS, tp), lambda b, hd: (hd, 0, b))],
        out_specs=pl.BlockSpec((tp, V_HEAD), lambda b, hd: (b, hd)),
        out_shape=jax.ShapeDtypeStruct((n, MLA_HEADS * V_HEAD), _BF16),
        scratch_shapes=[pltpu.VMEM((tp, TQ), _F32)] * 2,
        compiler_params=_params(("parallel", "parallel")),
    )(qt, k, vt)


def _lru_kernel(g_ref, x_ref, cw_ref, cb_ref, wg_ref, bg_ref, lam_ref, y_ref,
                xp_ref, af_ref, uf_ref, ab_ref, ub_ref, *, t, tp):
    bd = LRU_BLOCK_DIM
    n_chunk = SCAN_CHUNKS
    clen = t // n_chunk
    lam = lam_ref[...]
    sp = jnp.maximum(-lam, 0.0) + jnp.log1p(jnp.exp(-jnp.abs(lam)))

    for s in range(LRU_SLABS):
        lanes = slice(s * bd, (s + 1) * bd)
        xp_ref[0:CONV_HALO, :] = jnp.zeros((CONV_HALO, bd), _F32)
        xp_ref[CONV_HALO:CONV_HALO + tp, :] = x_ref[:, lanes].astype(_F32)
        xc = cb_ref[:, lanes]
        for tap in range(LRU_CONV):
            off = CONV_HALO + tap - LRU_CONV // 2
            xc = xc + xp_ref[off:off + t, :] * cw_ref[tap:tap + 1, lanes]

        th = jnp.tanh(_dot(xc.astype(_BF16), wg_ref[s]) + bg_ref[s])
        half_xc = 0.5 * xc
        for d, (a_ref, u_ref) in enumerate(((af_ref, uf_ref), (ab_ref, ub_ref))):
            th_r = th[:, d * bd:(d + 1) * bd]
            th_i = th[:, (2 + d) * bd:(3 + d) * bd]
            c = (-0.5 * LRU_C) * sp[d:d + 1, lanes]
            log_a = c * th_r + c
            a = jnp.exp(log_a)
            a_ref[s] = a
            u_ref[s] = _sqrt(_one_minus_sq(log_a, a)) * (half_xc * th_i + half_xc)

    def rows(j):
        return pl.ds(j, n_chunk, stride=clen)

    def pass1(j, carry):
        jb = clen - 1 - j
        out = []
        for s in range(LRU_SLABS):
            hf, pf, hb, pb = carry[4 * s:4 * s + 4]
            a = af_ref[s, rows(j), :]
            hf = a * hf + uf_ref[s, rows(j), :]
            pf = pf * a
            uf_ref[s, rows(j), :] = hf
            af_ref[s, rows(j), :] = pf
            a = ab_ref[s, rows(jb), :]
            hb = a * hb + ub_ref[s, rows(jb), :]
            pb = pb * a
            ub_ref[s, rows(jb), :] = hb
            ab_ref[s, rows(jb), :] = pb
            out += [hf, pf, hb, pb]
        return tuple(out)

    zero = jnp.zeros((n_chunk, bd), _F32)
    one = jnp.ones((n_chunk, bd), _F32)
    ends = lax.fori_loop(0, clen, pass1, (zero, one, zero, one) * LRU_SLABS)

    carries = []
    for s in range(LRU_SLABS):
        hf, pf, hb, pb = ends[4 * s:4 * s + 4]
        cf = [jnp.zeros((1, bd), _F32)]
        for c in range(1, n_chunk):
            cf.append(hf[c - 1:c, :] + pf[c - 1:c, :] * cf[c - 1])
        cb = [jnp.zeros((1, bd), _F32)]
        for c in range(n_chunk - 2, -1, -1):
            cb.insert(0, hb[c + 1:c + 2, :] + pb[c + 1:c + 2, :] * cb[0])
        carries.append((jnp.concatenate(cf, axis=0), jnp.concatenate(cb, axis=0)))

    def pass2(j, carry):
        for s in range(LRU_SLABS):
            cf, cb = carries[s]
            uf_ref[s, rows(j), :] = (uf_ref[s, rows(j), :] + af_ref[s, rows(j), :] * cf
                                     + ub_ref[s, rows(j), :] + ab_ref[s, rows(j), :] * cb)
        return carry

    lax.fori_loop(0, clen, pass2, 0, unroll=2)
    for s in range(LRU_SLABS):
        lanes = slice(s * bd, (s + 1) * bd)
        y_ref[0:t, lanes] = (jax.nn.gelu(g_ref[0:t, lanes].astype(_F32)) * uf_ref[s]).astype(y_ref.dtype)
    y_ref[t:tp, :] = jnp.zeros((tp - t, LRU_SLABS * bd), y_ref.dtype)


def _lru(gates, cw, cb, wg, bg, lam, bsz, t, tp):
    n = gates.shape[0]
    bd = LRU_BLOCK_DIM
    w = LRU_SLABS * bd
    nb = D_MODEL // w
    assert t % SCAN_CHUNKS == 0
    return pl.pallas_call(
        functools.partial(_lru_kernel, t=t, tp=tp),
        name="rg_lru",
        grid=(bsz, nb),
        in_specs=[pl.BlockSpec((tp, w), lambda b, g: (b, g)),
                  pl.BlockSpec((tp, w), lambda b, g: (b, nb + g)),
                  pl.BlockSpec((LRU_CONV, w), lambda b, g: (0, g)),
                  pl.BlockSpec((1, w), lambda b, g: (0, g)),
                  pl.BlockSpec((LRU_SLABS, bd, 4 * bd), lambda b, g: (g, 0, 0)),
                  pl.BlockSpec((LRU_SLABS, 1, 4 * bd), lambda b, g: (g, 0, 0)),
                  pl.BlockSpec((2, w), lambda b, g: (0, g))],
        out_specs=pl.BlockSpec((tp, w), lambda b, g: (b, g)),
        out_shape=jax.ShapeDtypeStruct((n, D_MODEL), _BF16),
        scratch_shapes=[pltpu.VMEM((tp + CONV_HALO, bd), _F32)] + [pltpu.VMEM((LRU_SLABS, t, bd), _F32)] * 4,
        compiler_params=_params(("parallel", "parallel")),
    )(gates, gates, cw, cb, wg, bg, lam)


def _mix_kernel(o_ref, y_ref, gm_ref, gl_ref, h_ref, wom_ref, wol_ref, wout_ref, g_ref, b_ref, out_ref, *, t, tp):
    tm = h_ref.shape[0]
    real = _real_rows(pl.program_id(0), tm, t, tp)
    rp = tm // MIX_PARTS
    for p in range(MIX_PARTS):
        rows = slice(p * rp, (p + 1) * rp)
        y_mla = _dot(o_ref[rows, :], wom_ref[...])
        y_lru = _dot(y_ref[rows, :], wol_ref[...])
        z = (jax.nn.sigmoid(gm_ref[rows, :].astype(_F32)) * y_mla
             + jax.nn.sigmoid(gl_ref[rows, :].astype(_F32)) * y_lru)
        r = DN_ALPHA * h_ref[rows, :] + _dot(z.astype(_BF16), wout_ref[...])
        out = _layer_norm(r, g_ref[...], b_ref[...])
        out_ref[rows, :] = jnp.where(real[rows], out, 0.0)


def _mix(o, y, gates, h, wom, wol, wout, g, b, t, tp):
    n, d = h.shape
    tm = TM_MIX
    tile = lambda i: (i, 0)
    return pl.pallas_call(
        functools.partial(_mix_kernel, t=t, tp=tp),
        name="mixer_out",
        grid=(n // tm,),
        in_specs=[pl.BlockSpec((tm, d), tile),
                  pl.BlockSpec((tm, d), tile),
                  pl.BlockSpec((tm, d), lambda i: (i, 2)),
                  pl.BlockSpec((tm, d), lambda i: (i, 3)),
                  pl.BlockSpec((tm, d), tile),
                  _resident((d, d)), _resident((d, d)), _resident((d, d)),
                  _resident((1, d)), _resident((1, d))],
        out_specs=pl.BlockSpec((tm, d), tile),
        out_shape=jax.ShapeDtypeStruct((n, d), _F32),
        compiler_params=_params(("parallel",)),
    )(o, y, gates, gates, h, wom, wol, wout, g, b)


def _ffn_core(lhs_ref, wu_ref, cw_ref, cb_ref, wd_ref, up0_ref, up1_ref, act_ref):
    halo = FFN_HALO
    tm = lhs_ref.shape[0] - 2 * halo
    d_ff = act_ref.shape[1]
    n_slab = up0_ref.shape[0] // 2
    ck = n_slab * LANES
    nj = d_ff // ck

    def up(j, up_ref):
        lhs = lhs_ref[...]
        for half in range(2):
            c0 = half * d_ff + j * ck
            res = _dot(lhs, wu_ref[:, c0:c0 + ck])
            for s in range(n_slab):
                up_ref[half * n_slab + s] = res[:, s * LANES:(s + 1) * LANES]

    def conv(up_ref, slab, c0):
        cols = slice(c0, c0 + LANES)
        out = cb_ref[:, cols]
        for tap in range(FFN_CONV):
            off = halo + tap - FFN_CONV // 2
            out = out + up_ref[slab, off:off + tm, :] * cw_ref[tap:tap + 1, cols]
        return out

    def act(j, up_ref):
        for s in range(n_slab):
            c0 = j * ck + s * LANES
            gate = conv(up_ref, s, c0)
            val = conv(up_ref, n_slab + s, d_ff + c0)
            act_ref[:, c0:c0 + LANES] = (jax.nn.gelu(gate) * val).astype(_BF16)

    bufs = (up0_ref, up1_ref)
    up(0, bufs[0])
    for j in range(nj):
        if j + 1 < nj:
            up(j + 1, bufs[(j + 1) % 2])
        act(j, bufs[j % 2])

    f = _dot(act_ref[:, 0:ck], wd_ref[0:ck, :])
    for j in range(1, nj):
        f = f + _dot(act_ref[:, j * ck:(j + 1) * ck], wd_ref[j * ck:(j + 1) * ck, :])
    return f


def _ffn_kernel(h_ref, hp_ref, hn_ref, wu_ref, cw_ref, cb_ref, wd_ref, g_ref, b_ref, out_ref,
                lhs_ref, up0_ref, up1_ref, act_ref, *, t, tp):
    tm = h_ref.shape[0]
    halo = FFN_HALO
    i = pl.program_id(0)
    lhs_ref[0:halo, :] = hp_ref[...].astype(_BF16)
    lhs_ref[halo:halo + tm, :] = h_ref[...].astype(_BF16)
    lhs_ref[halo + tm:, :] = hn_ref[...].astype(_BF16)

    @pl.when(i == 0)
    def _():
        lhs_ref[0:halo, :] = jnp.zeros((halo, lhs_ref.shape[1]), _BF16)

    f = _ffn_core(lhs_ref, wu_ref, cw_ref, cb_ref, wd_ref, up0_ref, up1_ref, act_ref)
    out = _layer_norm(DN_ALPHA * h_ref[...] + f, g_ref[...], b_ref[...])
    out_ref[...] = jnp.where(_real_rows(i, tm, t, tp), out, 0.0)


def _ffn_last_kernel(h_ref, hn_ref, wu_ref, cw_ref, cb_ref, wd_ref, g_ref, b_ref, out_ref,
                     lhs_ref, up0_ref, up1_ref, act_ref):
    tm = h_ref.shape[0]
    halo = FFN_HALO
    keep = tm - halo
    lhs_ref[0:tm, :] = h_ref[...].astype(_BF16)
    lhs_ref[tm:, :] = hn_ref[...].astype(_BF16)
    f = _ffn_core(lhs_ref, wu_ref, cw_ref, cb_ref, wd_ref, up0_ref, up1_ref, act_ref)
    g, b = g_ref[...], b_ref[...]
    out_ref[0:keep, :] = _layer_norm(DN_ALPHA * h_ref[halo:tm, :] + f[0:keep], g, b)
    out_ref[keep:tm, :] = _layer_norm(DN_ALPHA * hn_ref[0:halo, :] + f[keep:tm], g, b)


def _ffn(h, wu, cw, cb, wd, g, b, t, tp, last, bsz):
    n, d = h.shape
    tm = TM_FFN
    ck = FF_CHUNK
    d_ff = wd.shape[0]
    assert d_ff % ck == 0 and ck % LANES == 0
    halo = FFN_HALO
    weights = [_resident(wu.shape), _resident(cw.shape), _resident(cb.shape), _resident(wd.shape),
               _resident((1, d)), _resident((1, d))]
    up_scratch = pltpu.VMEM((2 * ck // LANES, tm + 2 * halo, LANES), _F32)
    scratch = [pltpu.VMEM((tm + 2 * halo, d), _BF16), up_scratch, up_scratch, pltpu.VMEM((tm, d_ff), _BF16)]
    if last:
        seq = t - N_META
        assert halo == N_META and seq % tm == 0 and tp >= seq + 2 * halo
        h3 = h.reshape(bsz, tp, d)
        return pl.pallas_call(
            _ffn_last_kernel,
            name="convffn_last",
            grid=(bsz, seq // tm),
            in_specs=[pl.BlockSpec((None, tm, d), lambda i, k: (i, k, 0)),
                      pl.BlockSpec((None, 2 * halo, d), lambda i, k: (i, (k + 1) * (tm // (2 * halo)), 0))] + weights,
            out_specs=pl.BlockSpec((None, tm, d), lambda i, k: (i, k, 0)),
            out_shape=jax.ShapeDtypeStruct((bsz, seq, d), _F32),
            scratch_shapes=scratch,
            compiler_params=_params(("parallel", "parallel")),
        )(h3, h3, wu, cw, cb, wd, g, b)
    hb = tm // halo
    last_halo = n // halo - 1
    return pl.pallas_call(
        functools.partial(_ffn_kernel, t=t, tp=tp),
        name="convffn",
        grid=(n // tm,),
        in_specs=[pl.BlockSpec((tm, d), lambda i: (i, 0)),
                  pl.BlockSpec((halo, d), lambda i: (jnp.maximum(i * hb - 1, 0), 0)),
                  pl.BlockSpec((halo, d), lambda i: (jnp.minimum((i + 1) * hb, last_halo), 0))] + weights,
        out_specs=pl.BlockSpec((tm, d), lambda i: (i, 0)),
        out_shape=jax.ShapeDtypeStruct((n, d), _F32),
        scratch_shapes=scratch,
        compiler_params=_params(("parallel",)),
    )(h, h, h, wu, cw, cb, wd, g, b)


def _swap_halves(w):
    half = w.shape[-1] // 2
    return jnp.concatenate([w[..., half:], w[..., :half]], axis=-1)


def kernel(x, meta_tokens, ln0_g, ln0_b, w_in, q_norm, kv_norm, w_uq, w_uk, w_uv, w_o_mla, lru_conv_w, lru_conv_b, w_rg, b_rg, w_ig, b_ig, lru_lambda, w_o_lru, w_out, ln1_g, ln1_b, w_up, ffn_conv_w, ffn_conv_b, w_down, ln2_g, ln2_b):
    bsz, seq, d = x.shape
    t = N_META + seq
    tp = -(-t // LANES) * LANES
    n = bsz * tp
    assert d == D_MODEL and tp > t >= max(TM_PROJ, TM_MIX, TM_FFN)
    assert n % TM_PROJ == 0 and n % TM_MIX == 0 and n % TM_FFN == 0
    row = lambda v: v.reshape(1, -1)

    h = _ln0(x, meta_tokens.astype(x.dtype), row(ln0_g), row(ln0_b), tp).reshape(n, d)

    half = QK_ROPE // 2
    inv_freq = jnp.exp(-math.log(ROPE_THETA) * jnp.arange(half, dtype=_F32) / half)
    ang = jnp.arange(tp, dtype=_F32)[:, None] * inv_freq[None, :]
    cos, sin = jnp.cos(ang), jnp.sin(ang)
    cs = jnp.tile(jnp.concatenate([cos, cos, -sin, sin], axis=-1), (bsz, 1))
    cst = jnp.tile(jnp.concatenate([cos, sin], axis=-1).T, (1, bsz))

    depth = w_in.shape[0]
    n_small = Q_RANK + KV_RANK + QK_ROPE
    w_in_b = w_in.astype(_BF16)
    ws_all = jnp.concatenate([w_in_b[..., :n_small], _swap_halves(w_in_b[..., Q_RANK + KV_RANK:n_small])], axis=-1)
    wg_all = w_in_b[..., n_small:]
    wqt_all = jnp.swapaxes(w_uq.astype(_BF16).reshape(depth, Q_RANK, MLA_HEADS * QK_DIM), 1, 2)
    wuk_all = w_uk.astype(_BF16).reshape(depth, KV_RANK, MLA_HEADS * QK_NOPE)
    wuvt_all = jnp.swapaxes(w_uv.astype(_BF16).reshape(depth, KV_RANK, MLA_HEADS * V_HEAD), 1, 2)
    w_gate_all = (0.5 * jnp.concatenate([w_rg[:, 0], w_rg[:, 1], w_ig[:, 0], w_ig[:, 1]], axis=-1)).astype(_BF16)
    b_gate_all = jnp.concatenate([b_rg.reshape(depth, 2, LRU_BLOCKS, LRU_BLOCK_DIM),
                                  b_ig.reshape(depth, 2, LRU_BLOCKS, LRU_BLOCK_DIM)], axis=1)
    b_gate_all = 0.5 * jnp.transpose(b_gate_all, (0, 2, 1, 3)).reshape(depth, LRU_BLOCKS, 1, 4 * LRU_BLOCK_DIM)
    wom_all, wol_all, wout_all = w_o_mla.astype(_BF16), w_o_lru.astype(_BF16), w_out.astype(_BF16)
    wup_all, wdown_all = w_up.astype(_BF16), w_down.astype(_BF16)

    for l in range(depth):
        qt, k, vt, gates = _proj(h, ws_all[l], wg_all[l], row(q_norm[l]), row(kv_norm[l]),
                                 wqt_all[l], wuk_all[l], wuvt_all[l], cs, cst)
        o = _attention(qt, k, vt, bsz, t, tp)
        y = _lru(gates, lru_conv_w[l], row(lru_conv_b[l]), w_gate_all[l], b_gate_all[l], lru_lambda[l], bsz, t, tp)
        h = _mix(o, y, gates, h, wom_all[l], wol_all[l], wout_all[l], row(ln1_g[l]), row(ln1_b[l]), t, tp)
        h = _ffn(h, wup_all[l], ffn_conv_w[l], row(ffn_conv_b[l]), wdown_all[l], row(ln2_g[l]), row(ln2_b[l]),
                 t, tp, last=(l == depth - 1), bsz=bsz)

    return h
```

```python
import functools
import math

import jax
import jax.numpy as jnp
from jax import lax
from jax.experimental import pallas as pl
from jax.experimental.pallas import tpu as pltpu

D_MODEL = 1024
N_META = 16
MLA_HEADS = 8
QK_NOPE = 128
QK_ROPE = 64
QK_DIM = QK_NOPE + QK_ROPE
V_HEAD = 128
Q_RANK = 256
KV_RANK = 128
ROPE_THETA = 10000.0
LRU_BLOCKS = 8
LRU_BLOCK_DIM = 128
LRU_CONV = 4
LRU_C = 8.0
D_FF = 2816
FFN_CONV = 3
DEPTH = 2
DN_ALPHA = (2.0 * DEPTH) ** 0.25
LN_EPS = 1e-5
RMS_EPS = 1e-6

SMALL_COLS = Q_RANK + KV_RANK + 2 * QK_ROPE
GATE_COLS = 4 * D_MODEL

LANES = 128
SUBLANES = 8
BF16_ROWS = 16
VMEM_LIMIT = 56 * 1024 * 1024

V_ROWS = V_HEAD + BF16_ROWS
TM_LN0 = 1024
TM_PROJ = 512
TM_MIX = 1024
MIX_PARTS = 4
TM_FFN = 512
FF_CHUNK = 256
TQ = 512
TK = 256
LRU_SLABS = 2
SCAN_CHUNKS = 2 * SUBLANES
CONV_HALO = SUBLANES
FFN_HALO = BF16_ROWS

_BF16 = jnp.bfloat16
_F32 = jnp.float32


def _params(sem):
    return pltpu.CompilerParams(dimension_semantics=sem, vmem_limit_bytes=VMEM_LIMIT)


def _resident(shape):
    zeros = (0,) * len(shape)
    return pl.BlockSpec(shape, lambda *_: zeros, pipeline_mode=pl.Buffered(1))


def _layer_norm(x, g, b):
    mu = jnp.mean(x, axis=-1, keepdims=True)
    xc = x - mu
    var = jnp.mean(xc * xc, axis=-1, keepdims=True)
    return xc * lax.rsqrt(var + LN_EPS) * g + b


def _rms_norm(x, g):
    ms = jnp.mean(x * x, axis=-1, keepdims=True)
    return x * lax.rsqrt(ms + RMS_EPS) * g


def _one_minus_sq(log_a, a):
    y = 2.0 * log_a
    u = a * a
    near_one = jnp.where(u == 1.0, y, (u - 1.0) * y * pl.reciprocal(jnp.log(u), approx=True))
    return -jnp.where(u > 0.5, near_one, u - 1.0)


def _sqrt(x):
    return jnp.where(x > 0.0, x * lax.rsqrt(x), 0.0)


def _dot(a, b):
    return jnp.dot(a, b, preferred_element_type=_F32)


def _dot_nt(a, b):
    return lax.dot_general(a, b, (((1,), (1,)), ((), ())), preferred_element_type=_F32)


def _real_rows(i, tm, t, tp):
    pos = lax.rem(i * tm, tp) + lax.broadcasted_iota(jnp.int32, (tm, 1), 0)
    return (pos < t) | (pos >= tp)


def _ln0_kernel(meta_ref, xprev_ref, x_ref, g_ref, b_ref, o_ref, *, t):
    k = pl.program_id(1)
    tm = o_ref.shape[0]
    g, b = g_ref[...], b_ref[...]
    head = jnp.where(k == 0, meta_ref[...], xprev_ref[...])
    o_ref[0:N_META, :] = _layer_norm(head, g, b)
    body = _layer_norm(x_ref[0:tm - N_META, :], g, b)
    pos = k * tm + N_META + lax.broadcasted_iota(jnp.int32, (tm - N_META, 1), 0)
    o_ref[N_META:tm, :] = jnp.where(pos < t, body, 0.0)


def _ln0(x, meta, g, b, tp):
    bsz, seq, d = x.shape
    tm = TM_LN0
    t = N_META + seq
    assert seq % tm == 0 and tm % N_META == 0 and N_META % SUBLANES == 0 and tp - seq <= tm
    per_tile = tm // N_META
    last_x = seq // tm - 1
    return pl.pallas_call(
        functools.partial(_ln0_kernel, t=t),
        name="ln0",
        grid=(bsz, pl.cdiv(tp, tm)),
        in_specs=[_resident((N_META, d)),
                  pl.BlockSpec((None, N_META, d), lambda i, k: (i, jnp.maximum(k * per_tile - 1, 0), 0)),
                  pl.BlockSpec((None, tm, d), lambda i, k: (i, jnp.minimum(k, last_x), 0)),
                  _resident((1, d)), _resident((1, d))],
        out_specs=pl.BlockSpec((None, tm, d), lambda i, k: (i, k, 0)),
        out_shape=jax.ShapeDtypeStruct((bsz, tp, d), _F32),
        compiler_params=_params(("parallel", "parallel")),
    )(meta, x, x, g, b)


def _proj_kernel(h_ref, w_ref, qn_ref, kvn_ref, wqt_ref, wuk_ref, wuvt_ref, cs_ref, cst_ref,
                 qt_ref, k_ref, vt_ref, g_ref):
    tm = h_ref.shape[0]
    hb = h_ref[...].astype(_BF16)
    small = _dot(hb, w_ref[:, 0:SMALL_COLS])
    cq = _rms_norm(small[:, :Q_RANK], qn_ref[...]).astype(_BF16)
    ckv = _rms_norm(small[:, Q_RANK:Q_RANK + KV_RANK], kvn_ref[...]).astype(_BF16)

    prod = small[:, Q_RANK + KV_RANK:] * cs_ref[...]
    k_rope = (prod + pltpu.roll(prod, QK_ROPE, axis=1))[:, :QK_ROPE].astype(_BF16)
    k_nope = _dot(ckv, wuk_ref[...])
    for hd in range(MLA_HEADS):
        k_ref[hd, :, :QK_NOPE] = k_nope[:, hd * QK_NOPE:(hd + 1) * QK_NOPE].astype(_BF16)
        k_ref[hd, :, QK_NOPE:] = k_rope

    half = QK_ROPE // 2
    qt = _dot_nt(wqt_ref[...], cq) * (1.0 / math.sqrt(QK_DIM))
    cos_t = cst_ref[0:half, :]
    sin_t = cst_ref[half:QK_ROPE, :]
    for hd in range(MLA_HEADS):
        r0 = hd * QK_DIM
        x1 = qt[r0 + QK_NOPE:r0 + QK_NOPE + half]
        x2 = qt[r0 + QK_NOPE + half:r0 + QK_DIM]
        qt_ref[hd, 0:QK_NOPE, :] = qt[r0:r0 + QK_NOPE].astype(_BF16)
        qt_ref[hd, QK_NOPE:QK_NOPE + half, :] = (x1 * cos_t - x2 * sin_t).astype(_BF16)
        qt_ref[hd, QK_NOPE + half:QK_DIM, :] = (x2 * cos_t + x1 * sin_t).astype(_BF16)

    vt = _dot_nt(wuvt_ref[...], ckv)
    row = lax.broadcasted_iota(jnp.int32, (V_ROWS - V_HEAD, tm), 0)
    ones_rows = jnp.where(row == 0, 1.0, 0.0).astype(_BF16)
    for hd in range(MLA_HEADS):
        vt_ref[hd, 0:V_HEAD, :] = vt[hd * V_HEAD:(hd + 1) * V_HEAD].astype(_BF16)
        vt_ref[hd, V_HEAD:V_ROWS, :] = ones_rows

    for c in range(GATE_COLS // D_MODEL):
        cols = slice(c * D_MODEL, (c + 1) * D_MODEL)
        w_cols = slice(SMALL_COLS + c * D_MODEL, SMALL_COLS + (c + 1) * D_MODEL)
        g_ref[:, cols] = _dot(hb, w_ref[:, w_cols]).astype(_BF16)


def _proj(h, w, qn, kvn, wqt, wuk, wuvt, cs, cst):
    n, d = h.shape
    assert w.shape == (d, SMALL_COLS + GATE_COLS) and SMALL_COLS % LANES == 0
    tm = TM_PROJ
    heads = lambda i: (0, i, 0)
    heads_t = lambda i: (0, 0, i)
    return pl.pallas_call(
        _proj_kernel,
        name="in_proj",
        grid=(n // tm,),
        in_specs=[pl.BlockSpec((tm, d), lambda i: (i, 0)),
                  _resident(w.shape), _resident(qn.shape), _resident(kvn.shape),
                  _resident(wqt.shape), _resident(wuk.shape), _resident(wuvt.shape),
                  pl.BlockSpec((tm, 2 * QK_ROPE), lambda i: (i, 0)),
                  pl.BlockSpec((QK_ROPE, tm), lambda i: (0, i))],
        out_specs=[pl.BlockSpec((MLA_HEADS, QK_DIM, tm), heads_t),
                   pl.BlockSpec((MLA_HEADS, tm, QK_DIM), heads),
                   pl.BlockSpec((MLA_HEADS, V_ROWS, tm), heads_t),
                   pl.BlockSpec((tm, GATE_COLS), lambda i: (i, 0))],
        out_shape=[jax.ShapeDtypeStruct((MLA_HEADS, QK_DIM, n), _BF16),
                   jax.ShapeDtypeStruct((MLA_HEADS, n, QK_DIM), _BF16),
                   jax.ShapeDtypeStruct((MLA_HEADS, V_ROWS, n), _BF16),
                   jax.ShapeDtypeStruct((n, GATE_COLS), _BF16)],
        compiler_params=_params(("parallel",)),
    )(h, w, qn, kvn, wqt, wuk, wuvt, cs, cst)


def _scores(k_ref, q_t, s_ref, t, tp):
    tq = q_t.shape[1]
    s_ref[:, 0:tq] = _dot(k_ref[...], q_t)
    s_ref[t:tp, 0:tq] = jnp.full((tp - t, tq), -jnp.inf, _F32)
    return jnp.max(s_ref[:, 0:tq], axis=0, keepdims=True)


def _values(vt_ref, s_ref, m, tq, tp):
    chunks = [(r0, min(TK, tp - r0)) for r0 in range(0, tp, TK)]
    acc = jnp.zeros((V_ROWS, tq), _F32)
    for r0, rows in chunks:
        p = jnp.exp(s_ref[r0:r0 + rows, 0:tq] - m).astype(_BF16)
        acc = acc + _dot(vt_ref[:, r0:r0 + rows], p)
    return acc[:V_HEAD] / acc[V_HEAD:V_HEAD + 1]


def _attn_kernel(qt_ref, k_ref, vt_ref, o_ref, s0_ref, s1_ref, *, t, tp):
    n_tiles = tp // TQ
    tail_q = tp - n_tiles * TQ

    def scores(start, tq, s_ref):
        return _scores(k_ref, qt_ref[:, pl.ds(start, tq)], s_ref, t, tp)

    def values(start, tq, s_ref, m):
        o_ref[pl.ds(start, tq), :] = _values(vt_ref, s_ref, m, tq, tp).T.astype(o_ref.dtype)

    def tile(i):
        return pl.multiple_of(i * TQ, TQ)

    m0 = scores(0, TQ, s0_ref)

    def body(j, m0):
        m1 = scores(tile(2 * j + 1), TQ, s1_ref)
        values(tile(2 * j), TQ, s0_ref, m0)
        m0 = scores(tile(2 * j + 2), TQ, s0_ref)
        values(tile(2 * j + 1), TQ, s1_ref, m1)
        return m0

    m0 = lax.fori_loop(0, n_tiles // 2 - 1, body, m0)
    m1 = scores((n_tiles - 1) * TQ, TQ, s1_ref)
    values((n_tiles - 2) * TQ, TQ, s0_ref, m0)
    m0 = scores(n_tiles * TQ, tail_q, s0_ref)
    values((n_tiles - 1) * TQ, TQ, s1_ref, m1)
    values(n_tiles * TQ, tail_q, s0_ref, m0)


def _attention(qt, k, vt, bsz, t, tp):
    n = k.shape[1]
    assert (tp // TQ) % 2 == 0 and 0 < tp % TQ <= TQ and tp % TQ % LANES == 0 and tp % TK % BF16_ROWS == 0
    return pl.pallas_call(
        functools.partial(_attn_kernel, t=t, tp=tp),
        name="mla_attention",
        grid=(bsz, MLA_HEADS),
        in_specs=[pl.BlockSpec((None, QK_DIM, tp), lambda b, hd: (hd, 0, b)),
                  pl.BlockSpec((None, tp, QK_DIM), lambda b, hd: (hd, b, 0)),
                  pl.BlockSpec((None, V_ROWS, tp), lambda b, hd: (hd, 0, b))],
        out_specs=pl.BlockSpec((tp, V_HEAD), lambda b, hd: (b, hd)),
        out_shape=jax.ShapeDtypeStruct((n, MLA_HEADS * V_HEAD), _BF16),
        scratch_shapes=[pltpu.VMEM((tp, TQ), _F32)] * 2,
        compiler_params=_params(("parallel", "parallel")),
    )(qt, k, vt)


def _lru_kernel(g_ref, x_ref, cw_ref, cb_ref, wg_ref, bg_ref, lam_ref, y_ref,
                xp_ref, af_ref, uf_ref, ab_ref, ub_ref, *, t, tp):
    bd = LRU_BLOCK_DIM
    n_chunk = SCAN_CHUNKS
    clen = t // n_chunk
    lam = lam_ref[...]
    sp = jnp.maximum(-lam, 0.0) + jnp.log1p(jnp.exp(-jnp.abs(lam)))

    for s in range(LRU_SLABS):
        lanes = slice(s * bd, (s + 1) * bd)
        xp_ref[0:CONV_HALO, :] = jnp.zeros((CONV_HALO, bd), _F32)
        xp_ref[CONV_HALO:CONV_HALO + tp, :] = x_ref[:, lanes].astype(_F32)
        xc = cb_ref[:, lanes]
        for tap in range(LRU_CONV):
            off = CONV_HALO + tap - LRU_CONV // 2
            xc = xc + xp_ref[off:off + t, :] * cw_ref[tap:tap + 1, lanes]

        th = jnp.tanh(_dot(xc.astype(_BF16), wg_ref[s]) + bg_ref[s])
        half_xc = 0.5 * xc
        for d, (a_ref, u_ref) in enumerate(((af_ref, uf_ref), (ab_ref, ub_ref))):
            th_r = th[:, d * bd:(d + 1) * bd]
            th_i = th[:, (2 + d) * bd:(3 + d) * bd]
            c = (-0.5 * LRU_C) * sp[d:d + 1, lanes]
            log_a = c * th_r + c
            a = jnp.exp(log_a)
            a_ref[s] = a
            u_ref[s] = _sqrt(_one_minus_sq(log_a, a)) * (half_xc * th_i + half_xc)

    def rows(j):
        return pl.ds(j, n_chunk, stride=clen)

    def pass1(j, carry):
        jb = clen - 1 - j
        out = []
        for s in range(LRU_SLABS):
            hf, pf, hb, pb = carry[4 * s:4 * s + 4]
            a = af_ref[s, rows(j), :]
            hf = a * hf + uf_ref[s, rows(j), :]
            pf = pf * a
            uf_ref[s, rows(j), :] = hf
            af_ref[s, rows(j), :] = pf
            a = ab_ref[s, rows(jb), :]
            hb = a * hb + ub_ref[s, rows(jb), :]
            pb = pb * a
            ub_ref[s, rows(jb), :] = hb
            ab_ref[s, rows(jb), :] = pb
            out += [hf, pf, hb, pb]
        return tuple(out)

    zero = jnp.zeros((n_chunk, bd), _F32)
    one = jnp.ones((n_chunk, bd), _F32)
    ends = lax.fori_loop(0, clen, pass1, (zero, one, zero, one) * LRU_SLABS)

    carries = []
    for s in range(LRU_SLABS):
        hf, pf, hb, pb = ends[4 * s:4 * s + 4]
        cf = [jnp.zeros((1, bd), _F32)]
        for c in range(1, n_chunk):
            cf.append(hf[c - 1:c, :] + pf[c - 1:c, :] * cf[c - 1])
        cb = [jnp.zeros((1, bd), _F32)]
        for c in range(n_chunk - 2, -1, -1):
            cb.insert(0, hb[c + 1:c + 2, :] + pb[c + 1:c + 2, :] * cb[0])
        carries.append((jnp.concatenate(cf, axis=0), jnp.concatenate(cb, axis=0)))

    def pass2(j, carry):
        for s in range(LRU_SLABS):
            cf, cb = carries[s]
            uf_ref[s, rows(j), :] = (uf_ref[s, rows(j), :] + af_ref[s, rows(j), :] * cf
                                     + ub_ref[s, rows(j), :] + ab_ref[s, rows(j), :] * cb)
        return carry

    lax.fori_loop(0, clen, pass2, 0, unroll=2)
    for s in range(LRU_SLABS):
        lanes = slice(s * bd, (s + 1) * bd)
        y_ref[0:t, lanes] = (jax.nn.gelu(g_ref[0:t, lanes].astype(_F32)) * uf_ref[s]).astype(y_ref.dtype)
    y_ref[t:tp, :] = jnp.zeros((tp - t, LRU_SLABS * bd), y_ref.dtype)


def _lru(gates, cw, cb, wg, bg, lam, bsz, t, tp):
    n = gates.shape[0]
    bd = LRU_BLOCK_DIM
    w = LRU_SLABS * bd
    nb = D_MODEL // w
    assert t % SCAN_CHUNKS == 0
    return pl.pallas_call(
        functools.partial(_lru_kernel, t=t, tp=tp),
        name="rg_lru",
        grid=(bsz, nb),
        in_specs=[pl.BlockSpec((tp, w), lambda b, g: (b, g)),
                  pl.BlockSpec((tp, w), lambda b, g: (b, nb + g)),
                  pl.BlockSpec((LRU_CONV, w), lambda b, g: (0, g)),
                  pl.BlockSpec((1, w), lambda b, g: (0, g)),
                  pl.BlockSpec((LRU_SLABS, bd, 4 * bd), lambda b, g: (g, 0, 0)),
                  pl.BlockSpec((LRU_SLABS, 1, 4 * bd), lambda b, g: (g, 0, 0)),
                  pl.BlockSpec((2, w), lambda b, g: (0, g))],
        out_specs=pl.BlockSpec((tp, w), lambda b, g: (b, g)),
        out_shape=jax.ShapeDtypeStruct((n, D_MODEL), _BF16),
        scratch_shapes=[pltpu.VMEM((tp + CONV_HALO, bd), _F32)] + [pltpu.VMEM((LRU_SLABS, t, bd), _F32)] * 4,
        compiler_params=_params(("parallel", "parallel")),
    )(gates, gates, cw, cb, wg, bg, lam)


def _mix_kernel(o_ref, y_ref, gm_ref, gl_ref, h_ref, wom_ref, wol_ref, wout_ref, g_ref, b_ref, out_ref, *, t, tp):
    tm = h_ref.shape[0]
    real = _real_rows(pl.program_id(0), tm, t, tp)
    rp = tm // MIX_PARTS
    for p in range(MIX_PARTS):
        rows = slice(p * rp, (p + 1) * rp)
        y_mla = _dot(o_ref[rows, :], wom_ref[...])
        y_lru = _dot(y_ref[rows, :], wol_ref[...])
        z = (jax.nn.sigmoid(gm_ref[rows, :].astype(_F32)) * y_mla
             + jax.nn.sigmoid(gl_ref[rows, :].astype(_F32)) * y_lru)
        r = DN_ALPHA * h_ref[rows, :] + _dot(z.astype(_BF16), wout_ref[...])
        out = _layer_norm(r, g_ref[...], b_ref[...])
        out_ref[rows, :] = jnp.where(real[rows], out, 0.0)


def _mix(o, y, gates, h, wom, wol, wout, g, b, t, tp):
    n, d = h.shape
    tm = TM_MIX
    tile = lambda i: (i, 0)
    return pl.pallas_call(
        functools.partial(_mix_kernel, t=t, tp=tp),
        name="mixer_out",
        grid=(n // tm,),
        in_specs=[pl.BlockSpec((tm, d), tile),
                  pl.BlockSpec((tm, d), tile),
                  pl.BlockSpec((tm, d), lambda i: (i, 2)),
                  pl.BlockSpec((tm, d), lambda i: (i, 3)),
                  pl.BlockSpec((tm, d), tile),
                  _resident((d, d)), _resident((d, d)), _resident((d, d)),
                  _resident((1, d)), _resident((1, d))],
        out_specs=pl.BlockSpec((tm, d), tile),
        out_shape=jax.ShapeDtypeStruct((n, d), _F32),
        compiler_params=_params(("parallel",)),
    )(o, y, gates, gates, h, wom, wol, wout, g, b)


def _ffn_core(lhs_ref, wu_ref, cw_ref, cb_ref, wd_ref, up0_ref, up1_ref, act_ref):
    halo = FFN_HALO
    tm = lhs_ref.shape[0] - 2 * halo
    d_ff = act_ref.shape[1]
    n_slab = up0_ref.shape[0] // 2
    ck = n_slab * LANES
    nj = d_ff // ck

    def up(j, up_ref):
        lhs = lhs_ref[...]
        for half in range(2):
            c0 = half * d_ff + j * ck
            res = _dot(lhs, wu_ref[:, c0:c0 + ck])
            for s in range(n_slab):
                up_ref[half * n_slab + s] = res[:, s * LANES:(s + 1) * LANES]

    def conv(up_ref, slab, c0):
        cols = slice(c0, c0 + LANES)
        out = cb_ref[:, cols]
        for tap in range(FFN_CONV):
            off = halo + tap - FFN_CONV // 2
            out = out + up_ref[slab, off:off + tm, :] * cw_ref[tap:tap + 1, cols]
        return out

    def act(j, up_ref):
        for s in range(n_slab):
            c0 = j * ck + s * LANES
            gate = conv(up_ref, s, c0)
            val = conv(up_ref, n_slab + s, d_ff + c0)
            act_ref[:, c0:c0 + LANES] = (jax.nn.gelu(gate) * val).astype(_BF16)

    bufs = (up0_ref, up1_ref)
    up(0, bufs[0])
    for j in range(nj):
        if j + 1 < nj:
            up(j + 1, bufs[(j + 1) % 2])
        act(j, bufs[j % 2])

    f = _dot(act_ref[:, 0:ck], wd_ref[0:ck, :])
    for j in range(1, nj):
        f = f + _dot(act_ref[:, j * ck:(j + 1) * ck], wd_ref[j * ck:(j + 1) * ck, :])
    return f


def _ffn_kernel(h_ref, hp_ref, hn_ref, wu_ref, cw_ref, cb_ref, wd_ref, g_ref, b_ref, out_ref,
                lhs_ref, up0_ref, up1_ref, act_ref, *, t, tp):
    tm = h_ref.shape[0]
    halo = FFN_HALO
    i = pl.program_id(0)
    lhs_ref[0:halo, :] = hp_ref[...].astype(_BF16)
    lhs_ref[halo:halo + tm, :] = h_ref[...].astype(_BF16)
    lhs_ref[halo + tm:, :] = hn_ref[...].astype(_BF16)

    @pl.when(i == 0)
    def _():
        lhs_ref[0:halo, :] = jnp.zeros((halo, lhs_ref.shape[1]), _BF16)

    f = _ffn_core(lhs_ref, wu_ref, cw_ref, cb_ref, wd_ref, up0_ref, up1_ref, act_ref)
    out = _layer_norm(DN_ALPHA * h_ref[...] + f, g_ref[...], b_ref[...])
    out_ref[...] = jnp.where(_real_rows(i, tm, t, tp), out, 0.0)


def _ffn_last_kernel(h_ref, hn_ref, wu_ref, cw_ref, cb_ref, wd_ref, g_ref, b_ref, out_ref,
                     lhs_ref, up0_ref, up1_ref, act_ref):
    tm = h_ref.shape[0]
    halo = FFN_HALO
    keep = tm - halo
    lhs_ref[0:tm, :] = h_ref[...].astype(_BF16)
    lhs_ref[tm:, :] = hn_ref[...].astype(_BF16)
    f = _ffn_core(lhs_ref, wu_ref, cw_ref, cb_ref, wd_ref, up0_ref, up1_ref, act_ref)
    g, b = g_ref[...], b_ref[...]
    out_ref[0:keep, :] = _layer_norm(DN_ALPHA * h_ref[halo:tm, :] + f[0:keep], g, b)
    out_ref[keep:tm, :] = _layer_norm(DN_ALPHA * hn_ref[0:halo, :] + f[keep:tm], g, b)


def _ffn(h, wu, cw, cb, wd, g, b, t, tp, last, bsz):
    n, d = h.shape
    tm = TM_FFN
    ck = FF_CHUNK
    d_ff = wd.shape[0]
    assert d_ff % ck == 0 and ck % LANES == 0
    halo = FFN_HALO
    weights = [_resident(wu.shape), _resident(cw.shape), _resident(cb.shape), _resident(wd.shape),
               _resident((1, d)), _resident((1, d))]
    up_scratch = pltpu.VMEM((2 * ck // LANES, tm + 2 * halo, LANES), _F32)
    scratch = [pltpu.VMEM((tm + 2 * halo, d), _BF16), up_scratch, up_scratch, pltpu.VMEM((tm, d_ff), _BF16)]
    if last:
        seq = t - N_META
        assert halo == N_META and seq % tm == 0 and tp >= seq + 2 * halo
        h3 = h.reshape(bsz, tp, d)
        return pl.pallas_call(
            _ffn_last_kernel,
            name="convffn_last",
            grid=(bsz, seq // tm),
            in_specs=[pl.BlockSpec((None, tm, d), lambda i, k: (i, k, 0)),
                      pl.BlockSpec((None, 2 * halo, d), lambda i, k: (i, (k + 1) * (tm // (2 * halo)), 0))] + weights,
            out_specs=pl.BlockSpec((None, tm, d), lambda i, k: (i, k, 0)),
            out_shape=jax.ShapeDtypeStruct((bsz, seq, d), _F32),
            scratch_shapes=scratch,
            compiler_params=_params(("parallel", "parallel")),
        )(h3, h3, wu, cw, cb, wd, g, b)
    hb = tm // halo
    last_halo = n // halo - 1
    return pl.pallas_call(
        functools.partial(_ffn_kernel, t=t, tp=tp),
        name="convffn",
        grid=(n // tm,),
        in_specs=[pl.BlockSpec((tm, d), lambda i: (i, 0)),
                  pl.BlockSpec((halo, d), lambda i: (jnp.maximum(i * hb - 1, 0), 0)),
                  pl.BlockSpec((halo, d), lambda i: (jnp.minimum((i + 1) * hb, last_halo), 0))] + weights,
        out_specs=pl.BlockSpec((tm, d), lambda i: (i, 0)),
        out_shape=jax.ShapeDtypeStruct((n, d), _F32),
        scratch_shapes=scratch,
        compiler_params=_params(("parallel",)),
    )(h, h, h, wu, cw, cb, wd, g, b)


def _swap_halves(w):
    half = w.shape[-1] // 2
    return jnp.concatenate([w[..., half:], w[..., :half]], axis=-1)


def kernel(x, meta_tokens, ln0_g, ln0_b, w_in, q_norm, kv_norm, w_uq, w_uk, w_uv, w_o_mla, lru_conv_w, lru_conv_b, w_rg, b_rg, w_ig, b_ig, lru_lambda, w_o_lru, w_out, ln1_g, ln1_b, w_up, ffn_conv_w, ffn_conv_b, w_down, ln2_g, ln2_b):
    bsz, seq, d = x.shape
    t = N_META + seq
    tp = -(-t // LANES) * LANES
    n = bsz * tp
    assert d == D_MODEL and tp > t >= max(TM_PROJ, TM_MIX, TM_FFN)
    assert n % TM_PROJ == 0 and n % TM_MIX == 0 and n % TM_FFN == 0
    row = lambda v: v.reshape(1, -1)

    h = _ln0(x, meta_tokens.astype(x.dtype), row(ln0_g), row(ln0_b), tp).reshape(n, d)

    half = QK_ROPE // 2
    inv_freq = jnp.exp(-math.log(ROPE_THETA) * jnp.arange(half, dtype=_F32) / half)
    ang = jnp.arange(tp, dtype=_F32)[:, None] * inv_freq[None, :]
    cos, sin = jnp.cos(ang), jnp.sin(ang)
    cs = jnp.tile(jnp.concatenate([cos, cos, -sin, sin], axis=-1), (bsz, 1))
    cst = jnp.tile(jnp.concatenate([cos, sin], axis=-1).T, (1, bsz))

    depth = w_in.shape[0]
    n_small = Q_RANK + KV_RANK + QK_ROPE
    w_in_b = w_in.astype(_BF16)
    wproj_all = jnp.concatenate([w_in_b[..., :n_small], _swap_halves(w_in_b[..., Q_RANK + KV_RANK:n_small]),
                                 w_in_b[..., n_small:]], axis=-1)
    wqt_all = jnp.swapaxes(w_uq.astype(_BF16).reshape(depth, Q_RANK, MLA_HEADS * QK_DIM), 1, 2)
    wuk_all = w_uk.astype(_BF16).reshape(depth, KV_RANK, MLA_HEADS * QK_NOPE)
    wuvt_all = jnp.swapaxes(w_uv.astype(_BF16).reshape(depth, KV_RANK, MLA_HEADS * V_HEAD), 1, 2)
    w_gate_all = (0.5 * jnp.concatenate([w_rg[:, 0], w_rg[:, 1], w_ig[:, 0], w_ig[:, 1]], axis=-1)).astype(_BF16)
    b_gate_all = jnp.concatenate([b_rg.reshape(depth, 2, LRU_BLOCKS, LRU_BLOCK_DIM),
                                  b_ig.reshape(depth, 2, LRU_BLOCKS, LRU_BLOCK_DIM)], axis=1)
    b_gate_all = 0.5 * jnp.transpose(b_gate_all, (0, 2, 1, 3)).reshape(depth, LRU_BLOCKS, 1, 4 * LRU_BLOCK_DIM)
    wom_all, wol_all, wout_all = w_o_mla.astype(_BF16), w_o_lru.astype(_BF16), w_out.astype(_BF16)
    wup_all, wdown_all = w_up.astype(_BF16), w_down.astype(_BF16)

    for l in range(depth):
        qt, k, vt, gates = _proj(h, wproj_all[l], row(q_norm[l]), row(kv_norm[l]),
                                 wqt_all[l], wuk_all[l], wuvt_all[l], cs, cst)
        o = _attention(qt, k, vt, bsz, t, tp)
        y = _lru(gates, lru_conv_w[l], row(lru_conv_b[l]), w_gate_all[l], b_gate_all[l], lru_lambda[l], bsz, t, tp)
        h = _mix(o, y, gates, h, wom_all[l], wol_all[l], wout_all[l], row(ln1_g[l]), row(ln1_b[l]), t, tp)
        h = _ffn(h, wup_all[l], ffn_conv_w[l], row(ffn_conv_b[l]), wdown_all[l], row(ln2_g[l]), row(ln2_b[l]),
                 t, tp, last=(l == depth - 1), bsz=bsz)

    return h
```

```python
import functools
import math

import jax
import jax.numpy as jnp
from jax import lax
from jax.experimental import pallas as pl
from jax.experimental.pallas import tpu as pltpu

D_MODEL = 1024
N_META = 16
MLA_HEADS = 8
QK_NOPE = 128
QK_ROPE = 64
QK_DIM = QK_NOPE + QK_ROPE
V_HEAD = 128
Q_RANK = 256
KV_RANK = 128
ROPE_THETA = 10000.0
LRU_BLOCKS = 8
LRU_BLOCK_DIM = 128
LRU_CONV = 4
LRU_C = 8.0
D_FF = 2816
FFN_CONV = 3
DEPTH = 2
DN_ALPHA = (2.0 * DEPTH) ** 0.25
LN_EPS = 1e-5
RMS_EPS = 1e-6

SMALL_COLS = Q_RANK + KV_RANK + 2 * QK_ROPE
GATE_COLS = 4 * D_MODEL

LANES = 128
SUBLANES = 8
BF16_ROWS = 16
VMEM_LIMIT = 56 * 1024 * 1024

V_ROWS = V_HEAD + BF16_ROWS
TM_LN0 = 1024
TM_PROJ = 512
TM_MIX = 1024
MIX_PARTS = 4
TM_FFN = 512
FF_CHUNK = 256
TQ = 512
TK = 256
LRU_SLABS = 2
SCAN_CHUNKS = 2 * SUBLANES
CONV_HALO = SUBLANES
FFN_HALO = BF16_ROWS

_BF16 = jnp.bfloat16
_F32 = jnp.float32


def _params(sem):
    return pltpu.CompilerParams(dimension_semantics=sem, vmem_limit_bytes=VMEM_LIMIT)


def _resident(shape):
    zeros = (0,) * len(shape)
    return pl.BlockSpec(shape, lambda *_: zeros, pipeline_mode=pl.Buffered(1))


def _layer_norm(x, g, b):
    mu = jnp.mean(x, axis=-1, keepdims=True)
    xc = x - mu
    var = jnp.mean(xc * xc, axis=-1, keepdims=True)
    return xc * lax.rsqrt(var + LN_EPS) * g + b


def _rms_norm(x, g):
    ms = jnp.mean(x * x, axis=-1, keepdims=True)
    return x * lax.rsqrt(ms + RMS_EPS) * g


def _one_minus_sq(log_a, a):
    y = 2.0 * log_a
    u = a * a
    near_one = jnp.where(u == 1.0, y, (u - 1.0) * y * pl.reciprocal(jnp.log(u), approx=True))
    return -jnp.where(u > 0.5, near_one, u - 1.0)


def _sqrt(x):
    return jnp.where(x > 0.0, x * lax.rsqrt(x), 0.0)


def _dot(a, b):
    return jnp.dot(a, b, preferred_element_type=_F32)


def _dot_nt(a, b):
    return lax.dot_general(a, b, (((1,), (1,)), ((), ())), preferred_element_type=_F32)


def _real_rows(i, tm, t, tp):
    pos = lax.rem(i * tm, tp) + lax.broadcasted_iota(jnp.int32, (tm, 1), 0)
    return (pos < t) | (pos >= tp)


def _ln0_kernel(meta_ref, xprev_ref, x_ref, g_ref, b_ref, o_ref, *, t):
    k = pl.program_id(1)
    tm = o_ref.shape[0]
    g, b = g_ref[...], b_ref[...]
    head = jnp.where(k == 0, meta_ref[...], xprev_ref[...])
    o_ref[0:N_META, :] = _layer_norm(head, g, b)
    body = _layer_norm(x_ref[0:tm - N_META, :], g, b)
    pos = k * tm + N_META + lax.broadcasted_iota(jnp.int32, (tm - N_META, 1), 0)
    o_ref[N_META:tm, :] = jnp.where(pos < t, body, 0.0)


def _ln0(x, meta, g, b, tp):
    bsz, seq, d = x.shape
    tm = TM_LN0
    t = N_META + seq
    assert seq % tm == 0 and tm % N_META == 0 and N_META % SUBLANES == 0 and tp - seq <= tm
    per_tile = tm // N_META
    last_x = seq // tm - 1
    return pl.pallas_call(
        functools.partial(_ln0_kernel, t=t),
        name="ln0",
        grid=(bsz, pl.cdiv(tp, tm)),
        in_specs=[_resident((N_META, d)),
                  pl.BlockSpec((None, N_META, d), lambda i, k: (i, jnp.maximum(k * per_tile - 1, 0), 0)),
                  pl.BlockSpec((None, tm, d), lambda i, k: (i, jnp.minimum(k, last_x), 0)),
                  _resident((1, d)), _resident((1, d))],
        out_specs=pl.BlockSpec((None, tm, d), lambda i, k: (i, k, 0)),
        out_shape=jax.ShapeDtypeStruct((bsz, tp, d), _F32),
        compiler_params=_params(("parallel", "parallel")),
    )(meta, x, x, g, b)


def _proj_kernel(h_ref, ws_ref, wg_ref, qn_ref, kvn_ref, wqt_ref, wuk_ref, wuvt_ref, cs_ref, cst_ref,
                 qt_ref, k_ref, vt_ref, g_ref):
    tm = h_ref.shape[0]
    hb = h_ref[...].astype(_BF16)
    small = _dot(hb, ws_ref[...])
    cq = _rms_norm(small[:, :Q_RANK], qn_ref[...]).astype(_BF16)
    ckv = _rms_norm(small[:, Q_RANK:Q_RANK + KV_RANK], kvn_ref[...]).astype(_BF16)

    prod = small[:, Q_RANK + KV_RANK:] * cs_ref[...]
    k_rope = (prod + pltpu.roll(prod, QK_ROPE, axis=1))[:, :QK_ROPE].astype(_BF16)
    k_nope = _dot(ckv, wuk_ref[...])
    for hd in range(MLA_HEADS):
        k_ref[hd, :, :QK_NOPE] = k_nope[:, hd * QK_NOPE:(hd + 1) * QK_NOPE].astype(_BF16)
        k_ref[hd, :, QK_NOPE:] = k_rope

    half = QK_ROPE // 2
    qt = _dot_nt(wqt_ref[...], cq) * (1.0 / math.sqrt(QK_DIM))
    cos_t = cst_ref[0:half, :]
    sin_t = cst_ref[half:QK_ROPE, :]
    for hd in range(MLA_HEADS):
        r0 = hd * QK_DIM
        x1 = qt[r0 + QK_NOPE:r0 + QK_NOPE + half]
        x2 = qt[r0 + QK_NOPE + half:r0 + QK_DIM]
        qt_ref[hd, 0:QK_NOPE, :] = qt[r0:r0 + QK_NOPE].astype(_BF16)
        qt_ref[hd, QK_NOPE:QK_NOPE + half, :] = (x1 * cos_t - x2 * sin_t).astype(_BF16)
        qt_ref[hd, QK_NOPE + half:QK_DIM, :] = (x2 * cos_t + x1 * sin_t).astype(_BF16)

    vt = _dot_nt(wuvt_ref[...], ckv)
    row = lax.broadcasted_iota(jnp.int32, (V_ROWS - V_HEAD, tm), 0)
    ones_rows = jnp.where(row == 0, 1.0, 0.0).astype(_BF16)
    for hd in range(MLA_HEADS):
        vt_ref[hd, 0:V_HEAD, :] = vt[hd * V_HEAD:(hd + 1) * V_HEAD].astype(_BF16)
        vt_ref[hd, V_HEAD:V_ROWS, :] = ones_rows

    for c in range(GATE_COLS // D_MODEL):
        cols = slice(c * D_MODEL, (c + 1) * D_MODEL)
        g_ref[:, cols] = _dot(hb, wg_ref[:, cols]).astype(_BF16)


def _proj(h, ws, wg, qn, kvn, wqt, wuk, wuvt, cs, cst):
    n, d = h.shape
    tm = TM_PROJ
    heads = lambda i: (0, i, 0)
    heads_t = lambda i: (0, 0, i)
    return pl.pallas_call(
        _proj_kernel,
        name="in_proj",
        grid=(n // tm,),
        in_specs=[pl.BlockSpec((tm, d), lambda i: (i, 0)),
                  _resident(ws.shape), _resident(wg.shape), _resident(qn.shape), _resident(kvn.shape),
                  _resident(wqt.shape), _resident(wuk.shape), _resident(wuvt.shape),
                  pl.BlockSpec((tm, 2 * QK_ROPE), lambda i: (i, 0)),
                  pl.BlockSpec((QK_ROPE, tm), lambda i: (0, i))],
        out_specs=[pl.BlockSpec((MLA_HEADS, QK_DIM, tm), heads_t),
                   pl.BlockSpec((MLA_HEADS, tm, QK_DIM), heads),
                   pl.BlockSpec((MLA_HEADS, V_ROWS, tm), heads_t),
                   pl.BlockSpec((tm, GATE_COLS), lambda i: (i, 0))],
        out_shape=[jax.ShapeDtypeStruct((MLA_HEADS, QK_DIM, n), _BF16),
                   jax.ShapeDtypeStruct((MLA_HEADS, n, QK_DIM), _BF16),
                   jax.ShapeDtypeStruct((MLA_HEADS, V_ROWS, n), _BF16),
                   jax.ShapeDtypeStruct((n, GATE_COLS), _BF16)],
        compiler_params=_params(("parallel",)),
    )(h, ws, wg, qn, kvn, wqt, wuk, wuvt, cs, cst)


def _scores(k_ref, q_t, s_ref, t, tp):
    tq = q_t.shape[1]
    s_ref[:, 0:tq] = _dot(k_ref[...], q_t)
    s_ref[t:tp, 0:tq] = jnp.full((tp - t, tq), -jnp.inf, _F32)
    return jnp.max(s_ref[:, 0:tq], axis=0, keepdims=True)


def _values(vt_ref, s_ref, m, tq, tp):
    chunks = [(r0, min(TK, tp - r0)) for r0 in range(0, tp, TK)]
    acc = jnp.zeros((V_ROWS, tq), _F32)
    for r0, rows in chunks:
        p = jnp.exp(s_ref[r0:r0 + rows, 0:tq] - m).astype(_BF16)
        acc = acc + _dot(vt_ref[:, r0:r0 + rows], p)
    return acc[:V_HEAD] / acc[V_HEAD:V_HEAD + 1]


def _attn_kernel(qt_ref, k_ref, vt_ref, o_ref, s0_ref, s1_ref, *, t, tp):
    n_tiles = tp // TQ
    tail_q = tp - n_tiles * TQ

    def scores(start, tq, s_ref):
        return _scores(k_ref, qt_ref[:, pl.ds(start, tq)], s_ref, t, tp)

    def values(start, tq, s_ref, m):
        o_ref[pl.ds(start, tq), :] = _values(vt_ref, s_ref, m, tq, tp).T.astype(o_ref.dtype)

    def tile(i):
        return pl.multiple_of(i * TQ, TQ)

    m0 = scores(0, TQ, s0_ref)

    def body(j, m0):
        m1 = scores(tile(2 * j + 1), TQ, s1_ref)
        values(tile(2 * j), TQ, s0_ref, m0)
        m0 = scores(tile(2 * j + 2), TQ, s0_ref)
        values(tile(2 * j + 1), TQ, s1_ref, m1)
        return m0

    m0 = lax.fori_loop(0, n_tiles // 2 - 1, body, m0)
    m1 = scores((n_tiles - 1) * TQ, TQ, s1_ref)
    values((n_tiles - 2) * TQ, TQ, s0_ref, m0)
    m0 = scores(n_tiles * TQ, tail_q, s0_ref)
    values((n_tiles - 1) * TQ, TQ, s1_ref, m1)
    values(n_tiles * TQ, tail_q, s0_ref, m0)


def _attention(qt, k, vt, bsz, t, tp):
    n = k.shape[1]
    assert (tp // TQ) % 2 == 0 and 0 < tp % TQ <= TQ and tp % TQ % LANES == 0 and tp % TK % BF16_ROWS == 0
    return pl.pallas_call(
        functools.partial(_attn_kernel, t=t, tp=tp),
        name="mla_attention",
        grid=(bsz, MLA_HEADS),
        in_specs=[pl.BlockSpec((None, QK_DIM, tp), lambda b, hd: (hd, 0, b)),
                  pl.BlockSpec((None, tp, QK_DIM), lambda b, hd: (hd, b, 0)),
                  pl.BlockSpec((None, V_ROWS, tp), lambda b, hd: (hd, 0, b))],
        out_specs=pl.BlockSpec((tp, V_HEAD), lambda b, hd: (b, hd)),
        out_shape=jax.ShapeDtypeStruct((n, MLA_HEADS * V_HEAD), _BF16),
        scratch_shapes=[pltpu.VMEM((tp, TQ), _F32)] * 2,
        compiler_params=_params(("parallel", "parallel")),
    )(qt, k, vt)


def _lru_kernel(g_ref, x_ref, cw_ref, cb_ref, wg_ref, bg_ref, lam_ref, y_ref,
                xp_ref, af_ref, uf_ref, ab_ref, ub_ref, *, t, tp):
    bd = LRU_BLOCK_DIM
    n_chunk = SCAN_CHUNKS
    clen = t // n_chunk
    lam = lam_ref[...]
    sp = jnp.maximum(-lam, 0.0) + jnp.log1p(jnp.exp(-jnp.abs(lam)))

    for s in range(LRU_SLABS):
        lanes = slice(s * bd, (s + 1) * bd)
        xp_ref[0:CONV_HALO, :] = jnp.zeros((CONV_HALO, bd), _F32)
        xp_ref[CONV_HALO:CONV_HALO + tp, :] = x_ref[:, lanes].astype(_F32)
        xc = cb_ref[:, lanes]
        for tap in range(LRU_CONV):
            off = CONV_HALO + tap - LRU_CONV // 2
            xc = xc + xp_ref[off:off + t, :] * cw_ref[tap:tap + 1, lanes]

        th = jnp.tanh(_dot(xc.astype(_BF16), wg_ref[s]) + bg_ref[s])
        half_xc = 0.5 * xc
        for d, (a_ref, u_ref) in enumerate(((af_ref, uf_ref), (ab_ref, ub_ref))):
            th_r = th[:, d * bd:(d + 1) * bd]
            th_i = th[:, (2 + d) * bd:(3 + d) * bd]
            c = (-0.5 * LRU_C) * sp[d:d + 1, lanes]
            log_a = c * th_r + c
            a = jnp.exp(log_a)
            a_ref[s] = a
            u_ref[s] = _sqrt(_one_minus_sq(log_a, a)) * (half_xc * th_i + half_xc)

    def rows(j):
        return pl.ds(j, n_chunk, stride=clen)

    def pass1(j, carry):
        jb = clen - 1 - j
        out = []
        for s in range(LRU_SLABS):
            hf, pf, hb, pb = carry[4 * s:4 * s + 4]
            a = af_ref[s, rows(j), :]
            hf = a * hf + uf_ref[s, rows(j), :]
            pf = pf * a
            uf_ref[s, rows(j), :] = hf
            af_ref[s, rows(j), :] = pf
            a = ab_ref[s, rows(jb), :]
            hb = a * hb + ub_ref[s, rows(jb), :]
            pb = pb * a
            ub_ref[s, rows(jb), :] = hb
            ab_ref[s, rows(jb), :] = pb
            out += [hf, pf, hb, pb]
        return tuple(out)

    zero = jnp.zeros((n_chunk, bd), _F32)
    one = jnp.ones((n_chunk, bd), _F32)
    ends = lax.fori_loop(0, clen, pass1, (zero, one, zero, one) * LRU_SLABS, unroll=2)

    carries = []
    for s in range(LRU_SLABS):
        hf, pf, hb, pb = ends[4 * s:4 * s + 4]
        cf = [jnp.zeros((1, bd), _F32)]
        for c in range(1, n_chunk):
            cf.append(hf[c - 1:c, :] + pf[c - 1:c, :] * cf[c - 1])
        cb = [jnp.zeros((1, bd), _F32)]
        for c in range(n_chunk - 2, -1, -1):
            cb.insert(0, hb[c + 1:c + 2, :] + pb[c + 1:c + 2, :] * cb[0])
        carries.append((jnp.concatenate(cf, axis=0), jnp.concatenate(cb, axis=0)))

    def pass2(j, carry):
        for s in range(LRU_SLABS):
            cf, cb = carries[s]
            uf_ref[s, rows(j), :] = (uf_ref[s, rows(j), :] + af_ref[s, rows(j), :] * cf
                                     + ub_ref[s, rows(j), :] + ab_ref[s, rows(j), :] * cb)
        return carry

    lax.fori_loop(0, clen, pass2, 0, unroll=2)
    for s in range(LRU_SLABS):
        lanes = slice(s * bd, (s + 1) * bd)
        y_ref[0:t, lanes] = (jax.nn.gelu(g_ref[0:t, lanes].astype(_F32)) * uf_ref[s]).astype(y_ref.dtype)
    y_ref[t:tp, :] = jnp.zeros((tp - t, LRU_SLABS * bd), y_ref.dtype)


def _lru(gates, cw, cb, wg, bg, lam, bsz, t, tp):
    n = gates.shape[0]
    bd = LRU_BLOCK_DIM
    w = LRU_SLABS * bd
    nb = D_MODEL // w
    assert t % SCAN_CHUNKS == 0
    return pl.pallas_call(
        functools.partial(_lru_kernel, t=t, tp=tp),
        name="rg_lru",
        grid=(bsz, nb),
        in_specs=[pl.BlockSpec((tp, w), lambda b, g: (b, g)),
                  pl.BlockSpec((tp, w), lambda b, g: (b, nb + g)),
                  pl.BlockSpec((LRU_CONV, w), lambda b, g: (0, g)),
                  pl.BlockSpec((1, w), lambda b, g: (0, g)),
                  pl.BlockSpec((LRU_SLABS, bd, 4 * bd), lambda b, g: (g, 0, 0)),
                  pl.BlockSpec((LRU_SLABS, 1, 4 * bd), lambda b, g: (g, 0, 0)),
                  pl.BlockSpec((2, w), lambda b, g: (0, g))],
        out_specs=pl.BlockSpec((tp, w), lambda b, g: (b, g)),
        out_shape=jax.ShapeDtypeStruct((n, D_MODEL), _BF16),
        scratch_shapes=[pltpu.VMEM((tp + CONV_HALO, bd), _F32)] + [pltpu.VMEM((LRU_SLABS, t, bd), _F32)] * 4,
        compiler_params=_params(("parallel", "parallel")),
    )(gates, gates, cw, cb, wg, bg, lam)


def _mix_kernel(o_ref, y_ref, gm_ref, gl_ref, h_ref, wom_ref, wol_ref, wout_ref, g_ref, b_ref, out_ref, *, t, tp):
    tm = h_ref.shape[0]
    real = _real_rows(pl.program_id(0), tm, t, tp)
    rp = tm // MIX_PARTS
    for p in range(MIX_PARTS):
        rows = slice(p * rp, (p + 1) * rp)
        y_mla = _dot(o_ref[rows, :], wom_ref[...])
        y_lru = _dot(y_ref[rows, :], wol_ref[...])
        z = (jax.nn.sigmoid(gm_ref[rows, :].astype(_F32)) * y_mla
             + jax.nn.sigmoid(gl_ref[rows, :].astype(_F32)) * y_lru)
        r = DN_ALPHA * h_ref[rows, :] + _dot(z.astype(_BF16), wout_ref[...])
        out = _layer_norm(r, g_ref[...], b_ref[...])
        out_ref[rows, :] = jnp.where(real[rows], out, 0.0)


def _mix(o, y, gates, h, wom, wol, wout, g, b, t, tp):
    n, d = h.shape
    tm = TM_MIX
    tile = lambda i: (i, 0)
    return pl.pallas_call(
        functools.partial(_mix_kernel, t=t, tp=tp),
        name="mixer_out",
        grid=(n // tm,),
        in_specs=[pl.BlockSpec((tm, d), tile),
                  pl.BlockSpec((tm, d), tile),
                  pl.BlockSpec((tm, d), lambda i: (i, 2)),
                  pl.BlockSpec((tm, d), lambda i: (i, 3)),
                  pl.BlockSpec((tm, d), tile),
                  _resident((d, d)), _resident((d, d)), _resident((d, d)),
                  _resident((1, d)), _resident((1, d))],
        out_specs=pl.BlockSpec((tm, d), tile),
        out_shape=jax.ShapeDtypeStruct((n, d), _F32),
        compiler_params=_params(("parallel",)),
    )(o, y, gates, gates, h, wom, wol, wout, g, b)


def _ffn_core(lhs_ref, wu_ref, cw_ref, cb_ref, wd_ref, up0_ref, up1_ref, act_ref):
    halo = FFN_HALO
    tm = lhs_ref.shape[0] - 2 * halo
    d_ff = act_ref.shape[1]
    n_slab = up0_ref.shape[0] // 2
    ck = n_slab * LANES
    nj = d_ff // ck

    def up(j, up_ref):
        lhs = lhs_ref[...]
        for half in range(2):
            c0 = half * d_ff + j * ck
            res = _dot(lhs, wu_ref[:, c0:c0 + ck])
            for s in range(n_slab):
                up_ref[half * n_slab + s] = res[:, s * LANES:(s + 1) * LANES]

    def conv(up_ref, slab, c0):
        cols = slice(c0, c0 + LANES)
        out = cb_ref[:, cols]
        for tap in range(FFN_CONV):
            off = halo + tap - FFN_CONV // 2
            out = out + up_ref[slab, off:off + tm, :] * cw_ref[tap:tap + 1, cols]
        return out

    def act(j, up_ref):
        for s in range(n_slab):
            c0 = j * ck + s * LANES
            gate = conv(up_ref, s, c0)
            val = conv(up_ref, n_slab + s, d_ff + c0)
            act_ref[:, c0:c0 + LANES] = (jax.nn.gelu(gate) * val).astype(_BF16)

    bufs = (up0_ref, up1_ref)
    up(0, bufs[0])
    for j in range(nj):
        if j + 1 < nj:
            up(j + 1, bufs[(j + 1) % 2])
        act(j, bufs[j % 2])

    f = _dot(act_ref[:, 0:ck], wd_ref[0:ck, :])
    for j in range(1, nj):
        f = f + _dot(act_ref[:, j * ck:(j + 1) * ck], wd_ref[j * ck:(j + 1) * ck, :])
    return f


def _ffn_kernel(h_ref, hp_ref, hn_ref, wu_ref, cw_ref, cb_ref, wd_ref, g_ref, b_ref, out_ref,
                lhs_ref, up0_ref, up1_ref, act_ref, *, t, tp):
    tm = h_ref.shape[0]
    halo = FFN_HALO
    i = pl.program_id(0)
    lhs_ref[0:halo, :] = hp_ref[...].astype(_BF16)
    lhs_ref[halo:halo + tm, :] = h_ref[...].astype(_BF16)
    lhs_ref[halo + tm:, :] = hn_ref[...].astype(_BF16)

    @pl.when(i == 0)
    def _():
        lhs_ref[0:halo, :] = jnp.zeros((halo, lhs_ref.shape[1]), _BF16)

    f = _ffn_core(lhs_ref, wu_ref, cw_ref, cb_ref, wd_ref, up0_ref, up1_ref, act_ref)
    out = _layer_norm(DN_ALPHA * h_ref[...] + f, g_ref[...], b_ref[...])
    out_ref[...] = jnp.where(_real_rows(i, tm, t, tp), out, 0.0)


def _ffn_last_kernel(h_ref, hn_ref, wu_ref, cw_ref, cb_ref, wd_ref, g_ref, b_ref, out_ref,
                     lhs_ref, up0_ref, up1_ref, act_ref):
    tm = h_ref.shape[0]
    halo = FFN_HALO
    keep = tm - halo
    lhs_ref[0:tm, :] = h_ref[...].astype(_BF16)
    lhs_ref[tm:, :] = hn_ref[...].astype(_BF16)
    f = _ffn_core(lhs_ref, wu_ref, cw_ref, cb_ref, wd_ref, up0_ref, up1_ref, act_ref)
    g, b = g_ref[...], b_ref[...]
    out_ref[0:keep, :] = _layer_norm(DN_ALPHA * h_ref[halo:tm, :] + f[0:keep], g, b)
    out_ref[keep:tm, :] = _layer_norm(DN_ALPHA * hn_ref[0:halo, :] + f[keep:tm], g, b)


def _ffn(h, wu, cw, cb, wd, g, b, t, tp, last, bsz):
    n, d = h.shape
    tm = TM_FFN
    ck = FF_CHUNK
    d_ff = wd.shape[0]
    assert d_ff % ck == 0 and ck % LANES == 0
    halo = FFN_HALO
    weights = [_resident(wu.shape), _resident(cw.shape), _resident(cb.shape), _resident(wd.shape),
               _resident((1, d)), _resident((1, d))]
    up_scratch = pltpu.VMEM((2 * ck // LANES, tm + 2 * halo, LANES), _F32)
    scratch = [pltpu.VMEM((tm + 2 * halo, d), _BF16), up_scratch, up_scratch, pltpu.VMEM((tm, d_ff), _BF16)]
    if last:
        seq = t - N_META
        assert halo == N_META and seq % tm == 0 and tp >= seq + 2 * halo
        h3 = h.reshape(bsz, tp, d)
        return pl.pallas_call(
            _ffn_last_kernel,
            name="convffn_last",
            grid=(bsz, seq // tm),
            in_specs=[pl.BlockSpec((None, tm, d), lambda i, k: (i, k, 0)),
                      pl.BlockSpec((None, 2 * halo, d), lambda i, k: (i, (k + 1) * (tm // (2 * halo)), 0))] + weights,
            out_specs=pl.BlockSpec((None, tm, d), lambda i, k: (i, k, 0)),
            out_shape=jax.ShapeDtypeStruct((bsz, seq, d), _F32),
            scratch_shapes=scratch,
            compiler_params=_params(("parallel", "parallel")),
        )(h3, h3, wu, cw, cb, wd, g, b)
    hb = tm // halo
    last_halo = n // halo - 1
    return pl.pallas_call(
        functools.partial(_ffn_kernel, t=t, tp=tp),
        name="convffn",
        grid=(n // tm,),
        in_specs=[pl.BlockSpec((tm, d), lambda i: (i, 0)),
                  pl.BlockSpec((halo, d), lambda i: (jnp.maximum(i * hb - 1, 0), 0)),
                  pl.BlockSpec((halo, d), lambda i: (jnp.minimum((i + 1) * hb, last_halo), 0))] + weights,
        out_specs=pl.BlockSpec((tm, d), lambda i: (i, 0)),
        out_shape=jax.ShapeDtypeStruct((n, d), _F32),
        scratch_shapes=scratch,
        compiler_params=_params(("parallel",)),
    )(h, h, h, wu, cw, cb, wd, g, b)


def _swap_halves(w):
    half = w.shape[-1] // 2
    return jnp.concatenate([w[..., half:], w[..., :half]], axis=-1)


def kernel(x, meta_tokens, ln0_g, ln0_b, w_in, q_norm, kv_norm, w_uq, w_uk, w_uv, w_o_mla, lru_conv_w, lru_conv_b, w_rg, b_rg, w_ig, b_ig, lru_lambda, w_o_lru, w_out, ln1_g, ln1_b, w_up, ffn_conv_w, ffn_conv_b, w_down, ln2_g, ln2_b):
    bsz, seq, d = x.shape
    t = N_META + seq
    tp = -(-t // LANES) * LANES
    n = bsz * tp
    assert d == D_MODEL and tp > t >= max(TM_PROJ, TM_MIX, TM_FFN)
    assert n % TM_PROJ == 0 and n % TM_MIX == 0 and n % TM_FFN == 0
    row = lambda v: v.reshape(1, -1)

    h = _ln0(x, meta_tokens.astype(x.dtype), row(ln0_g), row(ln0_b), tp).reshape(n, d)

    half = QK_ROPE // 2
    inv_freq = jnp.exp(-math.log(ROPE_THETA) * jnp.arange(half, dtype=_F32) / half)
    ang = jnp.arange(tp, dtype=_F32)[:, None] * inv_freq[None, :]
    cos, sin = jnp.cos(ang), jnp.sin(ang)
    cs = jnp.tile(jnp.concatenate([cos, cos, -sin, sin], axis=-1), (bsz, 1))
    cst = jnp.tile(jnp.concatenate([cos, sin], axis=-1).T, (1, bsz))

    depth = w_in.shape[0]
    n_small = Q_RANK + KV_RANK + QK_ROPE
    w_in_b = w_in.astype(_BF16)
    ws_all = jnp.concatenate([w_in_b[..., :n_small], _swap_halves(w_in_b[..., Q_RANK + KV_RANK:n_small])], axis=-1)
    wg_all = w_in_b[..., n_small:]
    wqt_all = jnp.swapaxes(w_uq.astype(_BF16).reshape(depth, Q_RANK, MLA_HEADS * QK_DIM), 1, 2)
    wuk_all = w_uk.astype(_BF16).reshape(depth, KV_RANK, MLA_HEADS * QK_NOPE)
    wuvt_all = jnp.swapaxes(w_uv.astype(_BF16).reshape(depth, KV_RANK, MLA_HEADS * V_HEAD), 1, 2)
    w_gate_all = (0.5 * jnp.concatenate([w_rg[:, 0], w_rg[:, 1], w_ig[:, 0], w_ig[:, 1]], axis=-1)).astype(_BF16)
    b_gate_all = jnp.concatenate([b_rg.reshape(depth, 2, LRU_BLOCKS, LRU_BLOCK_DIM),
                                  b_ig.reshape(depth, 2, LRU_BLOCKS, LRU_BLOCK_DIM)], axis=1)
    b_gate_all = 0.5 * jnp.transpose(b_gate_all, (0, 2, 1, 3)).reshape(depth, LRU_BLOCKS, 1, 4 * LRU_BLOCK_DIM)
    wom_all, wol_all, wout_all = w_o_mla.astype(_BF16), w_o_lru.astype(_BF16), w_out.astype(_BF16)
    wup_all, wdown_all = w_up.astype(_BF16), w_down.astype(_BF16)

    for l in range(depth):
        qt, k, vt, gates = _proj(h, ws_all[l], wg_all[l], row(q_norm[l]), row(kv_norm[l]),
                                 wqt_all[l], wuk_all[l], wuvt_all[l], cs, cst)
        o = _attention(qt, k, vt, bsz, t, tp)
        y = _lru(gates, lru_conv_w[l], row(lru_conv_b[l]), w_gate_all[l], b_gate_all[l], lru_lambda[l], bsz, t, tp)
        h = _mix(o, y, gates, h, wom_all[l], wol_all[l], wout_all[l], row(ln1_g[l]), row(ln1_b[l]), t, tp)
        h = _ffn(h, wup_all[l], ffn_conv_w[l], row(ffn_conv_b[l]), wdown_all[l], row(ln2_g[l]), row(ln2_b[l]),
                 t, tp, last=(l == depth - 1), bsz=bsz)

    return h
```
